```python
import jax, jax.numpy as jnp
from jax import lax
import numpy as np

D_MODEL = 1024
BATCH = 2
SEQ = 16384
DEPTH = 2
DEC_BATCH = 4
DEC_SEQ = 4096
PAST_LEN = 128

GRID_W = 64
N_MEM = 256
EPS = 1e-6
CONV_W = 1024
CONV_K = 3
NA_HEADS = 16
NA_HEAD_DIM = 64
NA_W = NA_HEADS * NA_HEAD_DIM
NA_WIN_ROWS = 8
NA_WIN_COLS = 16
LRU_W = 1024
LRU_BLOCKS = 8
LRU_BW = LRU_W // LRU_BLOCKS
LRU_CONV_K = 4
LRU_C = 8.0
SGU_W = 1024
SGU_GROUPS = 8
SGU_GD = SGU_W // SGU_GROUPS
SGU_CHUNK = 128
MEM_HEADS = 4
MEM_HEAD_DIM = 128
MEM_W = MEM_HEADS * MEM_HEAD_DIM
N_EVEN = (DEPTH + 1) // 2
N_ODD = DEPTH // 2
EVEN_SIZES = (CONV_W, CONV_W, CONV_W, CONV_W, NA_W, NA_W, NA_W, NA_W, MEM_W, MEM_W)
ODD_SIZES = (LRU_W, LRU_W, SGU_W, SGU_W, SGU_W, MEM_W, MEM_W)
EVEN_IN = sum(EVEN_SIZES)
ODD_IN = sum(ODD_SIZES)
EVEN_OUT = CONV_W + NA_W + MEM_W
ODD_OUT = LRU_W + SGU_W + MEM_W

kernel_name = "hybrid_bidir_conv_na_rglru_sgu_encoder"


def rmsnorm(x, g):
    xf = x.astype(jnp.float32)
    y = xf * lax.rsqrt(jnp.mean(xf * xf, axis=-1, keepdims=True) + EPS)
    return (y * g.astype(jnp.float32)).astype(x.dtype)


def split_cols(z, sizes):
    idx = [int(s) for s in np.cumsum(sizes)[:-1]]
    return jnp.split(z, idx, axis=-1)


def depthwise_conv(x, w, b, pad_left):
    K = w.shape[0]
    T = x.shape[1]
    xp = jnp.pad(x, ((0, 0), (pad_left, K - 1 - pad_left), (0, 0)))
    y = b
    for j in range(K):
        y = y + w[j] * xp[:, j:j + T]
    return y


def neighbourhood_attention(q, k, v, rpb):
    Bsz, T, H, Dh = q.shape
    rows = T // GRID_W
    wr = min(NA_WIN_ROWS, rows)
    qg = q.reshape(Bsz, rows, GRID_W, H, Dh) * (Dh ** -0.5)
    kg = k.reshape(Bsz, rows, GRID_W, H, Dh)
    vg = v.reshape(Bsz, rows, GRID_W, H, Dh)
    col = jnp.arange(GRID_W)
    cstart = jnp.clip(col - NA_WIN_COLS // 2, 0, GRID_W - NA_WIN_COLS)
    col_ok = (col[None, :] >= cstart[:, None]) & (col[None, :] < cstart[:, None] + NA_WIN_COLS)
    dc_idx = jnp.clip(col[None, :] - col[:, None], -(NA_WIN_COLS - 1), NA_WIN_COLS - 1) + NA_WIN_COLS - 1
    rpb32 = rpb.astype(jnp.float32)

    def row_block(r):
        rstart = jnp.clip(r - wr // 2, 0, rows - wr)
        k_blk = lax.dynamic_slice_in_dim(kg, rstart, wr, axis=1)
        v_blk = lax.dynamic_slice_in_dim(vg, rstart, wr, axis=1)
        q_r = lax.dynamic_index_in_dim(qg, r, axis=1, keepdims=False)
        s = jnp.einsum('bchd,bwkhd->bhcwk', q_r, k_blk).astype(jnp.float32)
        dr_idx = rstart + jnp.arange(wr) - r + NA_WIN_ROWS - 1
        bias = rpb32[:, dr_idx[None, :, None], dc_idx[:, None, :]]
        s = jnp.where(col_ok[None, None, :, None, :], s + bias[None], -jnp.inf)
        p = jax.nn.softmax(s, axis=(-2, -1)).astype(v.dtype)
        return jnp.einsum('bhcwk,bwkhd->bchd', p, v_blk)

    out = lax.map(row_block, jnp.arange(rows))
    return jnp.moveaxis(out, 0, 1).reshape(Bsz, T, H * Dh)


def memory_attention(q, mem_k, mem_v):
    Bsz, T = q.shape[:2]
    s = jnp.einsum('bthd,bmhd->bhtm', q, mem_k).astype(jnp.float32) * (MEM_HEAD_DIM ** -0.5)
    p = jax.nn.softmax(s, axis=-1).astype(q.dtype)
    return jnp.einsum('bhtm,bmhd->bthd', p, mem_v).reshape(Bsz, T, MEM_W)


def lru_combine(c1, c2):
    a1, b1 = c1
    a2, b2 = c2
    return a1 * a2, a2 * b1 + b2


def rglru_direction(xc, wa, ba, wi, bi, lam, reverse):
    Bsz, T, _ = xc.shape
    xb = xc.reshape(Bsz, T, LRU_BLOCKS, LRU_BW)
    r = jax.nn.sigmoid(jnp.einsum('btni,nij->btnj', xb, wa.astype(jnp.float32)).reshape(Bsz, T, LRU_W) + ba.astype(jnp.float32))
    i = jax.nn.sigmoid(jnp.einsum('btni,nij->btnj', xb, wi.astype(jnp.float32)).reshape(Bsz, T, LRU_W) + bi.astype(jnp.float32))
    log_a = -LRU_C * r * jax.nn.softplus(-lam.astype(jnp.float32))
    a = jnp.exp(log_a)
    b = jnp.sqrt(-jnp.expm1(2.0 * log_a)) * (i * xc)
    _, h = lax.associative_scan(lru_combine, (a, b), axis=1, reverse=reverse)
    return h


def spatial_gating(u, v, g, ws, bs):
    Bsz, T, _ = v.shape
    vn = rmsnorm(v, g).reshape(Bsz, T // SGU_CHUNK, SGU_CHUNK, SGU_GROUPS, SGU_GD)
    mixed = jnp.einsum('gpq,bnqgd->bnpgd', ws, vn) + bs.T[None, None, :, :, None]
    return u * mixed.reshape(Bsz, T, SGU_W)


def even_mixer(h, mem_k, mem_v, w_in, conv_w, conv_b, rpb, w_out):
    Bsz, T, _ = h.shape
    bg, cg, xv, ga, q, k, v, gb, qm, gm = split_cols(h @ w_in, EVEN_SIZES)
    ya = bg * depthwise_conv(cg * xv, conv_w, conv_b, CONV_K // 2) * jax.nn.silu(ga)
    hs = (Bsz, T, NA_HEADS, NA_HEAD_DIM)
    yb = neighbourhood_attention(q.reshape(hs), k.reshape(hs), v.reshape(hs), rpb) * jax.nn.silu(gb)
    ym = memory_attention(qm.reshape(Bsz, T, MEM_HEADS, MEM_HEAD_DIM), mem_k, mem_v) * jax.nn.silu(gm)
    return jnp.concatenate([ya, yb, ym], axis=-1) @ w_out


def odd_mixer(h, mem_k, mem_v, w_in, conv_w, conv_b, lru_wa, lru_ba, lru_wi, lru_bi, lru_lam,
              sgu_g, sgu_w, sgu_b, w_out):
    Bsz, T, _ = h.shape
    xr, gc, u, v, gd, qm, gm = split_cols(h @ w_in, ODD_SIZES)
    xc = depthwise_conv(xr, conv_w, conv_b, LRU_CONV_K // 2).astype(jnp.float32)
    hc = (rglru_direction(xc, lru_wa[0], lru_ba[0], lru_wi[0], lru_bi[0], lru_lam[0], False)
          + rglru_direction(xc, lru_wa[1], lru_ba[1], lru_wi[1], lru_bi[1], lru_lam[1], True))
    yc = hc.astype(h.dtype) * jax.nn.silu(gc)
    yd = spatial_gating(u, v, sgu_g, sgu_w, sgu_b) * jax.nn.silu(gd)
    ym = memory_attention(qm.reshape(Bsz, T, MEM_HEADS, MEM_HEAD_DIM), mem_k, mem_v) * jax.nn.silu(gm)
    return jnp.concatenate([yc, yd, ym], axis=-1) @ w_out


def trunk(x, mem, norm_g, mem_norm_g, w_mem_kv, ev_w_in, ev_conv_w, ev_conv_b, ev_rpb, ev_w_out,
          od_w_in, od_conv_w, od_conv_b, od_lru_wa, od_lru_ba, od_lru_wi, od_lru_bi, od_lru_lam,
          od_sgu_g, od_sgu_w, od_sgu_b, od_w_out, final_g):
    Bsz, n_mem, _ = mem.shape
    for layer in range(DEPTH):
        mk, mv = split_cols(rmsnorm(mem, mem_norm_g[layer]) @ w_mem_kv[layer], (MEM_W, MEM_W))
        mk = mk.reshape(Bsz, n_mem, MEM_HEADS, MEM_HEAD_DIM)
        mv = mv.reshape(Bsz, n_mem, MEM_HEADS, MEM_HEAD_DIM)
        h = rmsnorm(x, norm_g[layer])
        if layer % 2 == 0:
            e = layer // 2
            x = x + even_mixer(h, mk, mv, ev_w_in[e], ev_conv_w[e], ev_conv_b[e], ev_rpb[e], ev_w_out[e])
        else:
            o = layer // 2
            x = x + odd_mixer(h, mk, mv, od_w_in[o], od_conv_w[o], od_conv_b[o], od_lru_wa[o], od_lru_ba[o],
                              od_lru_wi[o], od_lru_bi[o], od_lru_lam[o], od_sgu_g[o], od_sgu_w[o], od_sgu_b[o],
                              od_w_out[o])
    return rmsnorm(x, final_g)


def setup_inputs(seed: int = 0) -> dict:
    key = jax.random.key(seed)
    ks = jax.random.split(key, 32)
    f32 = jnp.float32

    def nrm(k, shape, scale):
        return jax.random.normal(k, shape, f32) * scale

    a0 = jax.random.uniform(ks[20], (N_ODD, 2, LRU_W), f32, minval=0.9, maxval=0.999)
    return {
        "x_prompt": nrm(ks[0], (BATCH, SEQ, D_MODEL), 1.0),
        "x_sample": nrm(ks[1], (DEC_BATCH, DEC_SEQ, D_MODEL), 1.0),
        "mem_prompt": nrm(ks[2], (BATCH, N_MEM, D_MODEL), 1.0),
        "mem_sample": nrm(ks[3], (DEC_BATCH, N_MEM, D_MODEL), 1.0),
        "norm_g": 1.0 + nrm(ks[4], (DEPTH, D_MODEL), 0.01),
        "mem_norm_g": 1.0 + nrm(ks[5], (DEPTH, D_MODEL), 0.01),
        "w_mem_kv": nrm(ks[6], (DEPTH, D_MODEL, 2 * MEM_W), D_MODEL ** -0.5),
        "ev_w_in": nrm(ks[7], (N_EVEN, D_MODEL, EVEN_IN), D_MODEL ** -0.5),
        "ev_conv_w": nrm(ks[8], (N_EVEN, CONV_K, CONV_W), CONV_K ** -0.5),
        "ev_conv_b": nrm(ks[9], (N_EVEN, CONV_W), 0.01),
        "ev_rpb": nrm(ks[10], (N_EVEN, NA_HEADS, 2 * NA_WIN_ROWS - 1, 2 * NA_WIN_COLS - 1), 0.1),
        "ev_w_out": nrm(ks[11], (N_EVEN, EVEN_OUT, D_MODEL), EVEN_OUT ** -0.5),
        "od_w_in": nrm(ks[12], (N_ODD, D_MODEL, ODD_IN), D_MODEL ** -0.5),
        "od_conv_w": nrm(ks[13], (N_ODD, LRU_CONV_K, LRU_W), LRU_CONV_K ** -0.5),
        "od_conv_b": nrm(ks[14], (N_ODD, LRU_W), 0.01),
        "od_lru_wa": nrm(ks[15], (N_ODD, 2, LRU_BLOCKS, LRU_BW, LRU_BW), LRU_BW ** -0.5),
        "od_lru_ba": nrm(ks[16], (N_ODD, 2, LRU_W), 0.1),
        "od_lru_wi": nrm(ks[17], (N_ODD, 2, LRU_BLOCKS, LRU_BW, LRU_BW), LRU_BW ** -0.5),
        "od_lru_bi": nrm(ks[18], (N_ODD, 2, LRU_W), 0.1),
        "od_lru_lam": jnp.log(a0) - jnp.log1p(-a0),
        "od_sgu_g": 1.0 + nrm(ks[21], (N_ODD, SGU_W), 0.01),
        "od_sgu_w": nrm(ks[22], (N_ODD, SGU_GROUPS, SGU_CHUNK, SGU_CHUNK), SGU_CHUNK ** -0.5),
        "od_sgu_b": 1.0 + nrm(ks[23], (N_ODD, SGU_GROUPS, SGU_CHUNK), 0.01),
        "od_w_out": nrm(ks[24], (N_ODD, ODD_OUT, D_MODEL), ODD_OUT ** -0.5),
        "final_g": 1.0 + nrm(ks[25], (D_MODEL,), 0.01),
    }


def reference(x_prompt, x_sample, mem_prompt, mem_sample, norm_g, mem_norm_g, w_mem_kv, ev_w_in, ev_conv_w,
              ev_conv_b, ev_rpb, ev_w_out, od_w_in, od_conv_w, od_conv_b, od_lru_wa, od_lru_ba, od_lru_wi,
              od_lru_bi, od_lru_lam, od_sgu_g, od_sgu_w, od_sgu_b, od_w_out, final_g):
    y_prompt = trunk(x_prompt, mem_prompt, norm_g, mem_norm_g, w_mem_kv, ev_w_in, ev_conv_w, ev_conv_b, ev_rpb,
                     ev_w_out, od_w_in, od_conv_w, od_conv_b, od_lru_wa, od_lru_ba, od_lru_wi, od_lru_bi,
                     od_lru_lam, od_sgu_g, od_sgu_w, od_sgu_b, od_w_out, final_g)
    y_sample = trunk(x_sample, mem_sample, norm_g, mem_norm_g, w_mem_kv, ev_w_in, ev_conv_w, ev_conv_b, ev_rpb,
                     ev_w_out, od_w_in, od_conv_w, od_conv_b, od_lru_wa, od_lru_ba, od_lru_wi, od_lru_bi,
                     od_lru_lam, od_sgu_g, od_sgu_w, od_sgu_b, od_w_out, final_g)
    return (y_prompt, y_sample)
```

```python
import functools

import jax
import jax.numpy as jnp
from jax import lax
from jax.experimental import pallas as pl
from jax.experimental.pallas import tpu as pltpu

F32 = jnp.float32
BF16 = jnp.bfloat16

D_MODEL = 1024
GRID_W = 64
N_MEM = 256
EPS = 1e-6
CONV_K = 3
NA_HEADS = 16
NA_HEAD_DIM = 64
NA_WIN_ROWS = 8
NA_WIN_COLS = 16
LRU_BLOCKS = 8
LRU_BW = 128
LRU_CONV_K = 4
LRU_C = 8.0
SGU_GROUPS = 8
SGU_CHUNK = 128
MEM_HEADS = 4
MEM_HEAD_DIM = 128
MEM_W = MEM_HEADS * MEM_HEAD_DIM

EV_BG, EV_CG, EV_XV, EV_GA, EV_Q, EV_K, EV_V, EV_GB, EV_QM, EV_GM = (
    0, 1024, 2048, 3072, 4096, 5120, 6144, 7168, 8192, 8704)
OD_XR, OD_GC, OD_U, OD_V, OD_GD, OD_QM, OD_GM = (0, 1024, 2048, 3072, 4096, 5120, 5632)

V7X_VMEM_LIMIT_BYTES = 56 * 1024 * 1024
SUBLANES = 8
HALO = 16
TOKEN_TILE = 512
NA_ROW_BLOCK = 8
NA_PAIRS = NA_HEADS // 2
MASK_BIAS = -1e30


def _dot(a, b):
    return jnp.dot(a, b, preferred_element_type=F32)


def _dot_nt(a, b):
    return lax.dot_general(a, b, (((1,), (1,)), ((), ())), preferred_element_type=F32)


def _rms(x, g):
    return x * lax.rsqrt(jnp.mean(x * x, axis=-1, keepdims=True) + EPS) * g


def _sigmoid(x):
    return 1.0 / (1.0 + jnp.exp(-x))


def _silu(x):
    return x * _sigmoid(x)


def _mem_attention(qm, gm, kv_ref, out_ref):
    scale = MEM_HEAD_DIM ** -0.5
    for hh in range(MEM_HEADS):
        lo, hi = hh * MEM_HEAD_DIM, (hh + 1) * MEM_HEAD_DIM
        s = _dot_nt(qm[:, lo:hi].astype(BF16), kv_ref[0, :, lo:hi]) * scale
        e = jnp.exp(s - jnp.max(s, axis=-1, keepdims=True))
        o = _dot(e.astype(BF16), kv_ref[0, :, MEM_W + lo:MEM_W + hi])
        o = o / jnp.sum(e, axis=-1, keepdims=True)
        out_ref[0, :, lo:hi] = (o * _silu(gm[:, lo:hi])).astype(out_ref.dtype)


def _normed_with_halo(x_ref, xp_ref, xn_ref, g):
    h = _rms(x_ref[0], g).astype(BF16)
    hp = _rms(xp_ref[0], g).astype(BF16)
    hn = _rms(xn_ref[0], g).astype(BF16)
    return h, jnp.concatenate([hp, h, hn], axis=0)


def _halo_valid(tt):
    i = pl.program_id(1)
    last = pl.num_programs(1) - 1
    row = lax.broadcasted_iota(jnp.int32, (tt + 2 * HALO, 1), 0)
    return ((row >= HALO) | (i > 0)) & ((row < tt + HALO) | (i < last))


def _memkv_kernel(mem_ref, g_ref, w_ref, kv_ref):
    hn = _rms(mem_ref[...], g_ref[0]).astype(BF16)
    kv_ref[0] = _dot(hn, w_ref[0]).astype(BF16)


def _memkv(mem_all, mem_norm_g, w_mem_kv):
    depth = w_mem_kv.shape[0]
    nb = mem_all.shape[0] // N_MEM
    return pl.pallas_call(
        _memkv_kernel,
        grid=(depth, nb),
        in_specs=[
            pl.BlockSpec((N_MEM, D_MODEL), lambda l, b: (b, 0)),
            pl.BlockSpec((1, 1, D_MODEL), lambda l, b: (l, 0, 0)),
            pl.BlockSpec((1, D_MODEL, 2 * MEM_W), lambda l, b: (l, 0, 0)),
        ],
        out_specs=pl.BlockSpec((1, N_MEM, 2 * MEM_W), lambda l, b: (l, b, 0)),
        out_shape=jax.ShapeDtypeStruct((depth, nb * N_MEM, 2 * MEM_W), BF16),
        compiler_params=pltpu.CompilerParams(
            dimension_semantics=("arbitrary", "arbitrary"), vmem_limit_bytes=V7X_VMEM_LIMIT_BYTES),
        name="memkv",
    )(mem_all, mem_norm_g.reshape(depth, 1, D_MODEL), w_mem_kv)


def _even_front_kernel(x_ref, xp_ref, xn_ref, g_ref, w_ref, cw_ref, cb_ref, kv_ref,
                       ya_ref, q_ref, k_ref, v_ref, sgb_ref, ym_ref, p_scr):
    tt = x_ref.shape[1]
    h, h_ext = _normed_with_halo(x_ref, xp_ref, xn_ref, g_ref[...])

    cg = _dot(h_ext, w_ref[:, EV_CG:EV_CG + D_MODEL])
    xv = _dot(h_ext, w_ref[:, EV_XV:EV_XV + D_MODEL])
    p_scr[...] = jnp.where(_halo_valid(tt), cg * xv, 0.0)
    conv = cb_ref[...]
    for j in range(CONV_K):
        conv = conv + cw_ref[j:j + 1, :] * p_scr[pl.ds(HALO - CONV_K // 2 + j, tt), :]
    bg = _dot(h, w_ref[:, EV_BG:EV_BG + D_MODEL])
    ga = _dot(h, w_ref[:, EV_GA:EV_GA + D_MODEL])
    ya_ref[0] = (bg * conv * _silu(ga)).astype(BF16)

    q_ref[0] = (_dot(h, w_ref[:, EV_Q:EV_Q + D_MODEL]) * (NA_HEAD_DIM ** -0.5)).astype(BF16)
    k_ref[0] = _dot(h, w_ref[:, EV_K:EV_K + D_MODEL]).astype(BF16)
    v_ref[0] = _dot(h, w_ref[:, EV_V:EV_V + D_MODEL]).astype(BF16)
    sgb_ref[0] = _silu(_dot(h, w_ref[:, EV_GB:EV_GB + D_MODEL])).astype(BF16)

    qm = _dot(h, w_ref[:, EV_QM:EV_QM + MEM_W])
    gm = _dot(h, w_ref[:, EV_GM:EV_GM + MEM_W])
    _mem_attention(qm, gm, kv_ref, ym_ref)


def _tile_specs(tt, t_len):
    per = tt // HALO
    n_halo = t_len // HALO
    return [
        pl.BlockSpec((1, tt, D_MODEL), lambda b, i: (b, i, 0)),
        pl.BlockSpec((1, HALO, D_MODEL), lambda b, i: (b, jnp.maximum(i * per - 1, 0), 0)),
        pl.BlockSpec((1, HALO, D_MODEL), lambda b, i: (b, jnp.minimum((i + 1) * per, n_halo - 1), 0)),
    ]


def _const_spec(shape):
    nd = len(shape)
    return pl.BlockSpec(shape, lambda b, i: (0,) * nd, pipeline_mode=pl.Buffered(1))


def _even_front(x, g, w_in, conv_w, conv_b, kv):
    bsz, t_len, _ = x.shape
    tt = min(TOKEN_TILE, t_len)
    act = lambda width: pl.BlockSpec((1, tt, width), lambda b, i: (b, i, 0))
    out_sd = lambda width: jax.ShapeDtypeStruct((bsz, t_len, width), BF16)
    return pl.pallas_call(
        _even_front_kernel,
        grid=(bsz, t_len // tt),
        in_specs=_tile_specs(tt, t_len) + [
            _const_spec((1, D_MODEL)),
            _const_spec(w_in.shape),
            _const_spec(conv_w.shape),
            _const_spec((1, D_MODEL)),
            pl.BlockSpec((1, N_MEM, 2 * MEM_W), lambda b, i: (b, 0, 0)),
        ],
        out_specs=[act(D_MODEL)] * 5 + [act(MEM_W)],
        out_shape=[out_sd(D_MODEL)] * 5 + [out_sd(MEM_W)],
        scratch_shapes=[pltpu.VMEM((tt + 2 * HALO, D_MODEL), F32)],
        compiler_params=pltpu.CompilerParams(
            dimension_semantics=("arbitrary", "arbitrary"), vmem_limit_bytes=V7X_VMEM_LIMIT_BYTES),
        name="even_front",
    )(x, x, x, g.reshape(1, D_MODEL), w_in, conv_w, conv_b.reshape(1, D_MODEL), kv)


def _na_bias_table(rpb):
    col = jnp.arange(GRID_W)
    cstart = jnp.clip(col - NA_WIN_COLS // 2, 0, GRID_W - NA_WIN_COLS)
    col_ok = (col[None, :] >= cstart[:, None]) & (col[None, :] < cstart[:, None] + NA_WIN_COLS)
    dc_idx = jnp.clip(col[None, :] - col[:, None], -(NA_WIN_COLS - 1), NA_WIN_COLS - 1) + NA_WIN_COLS - 1
    per_dr = jnp.where(col_ok[None, None], rpb.astype(F32)[:, :, dc_idx], MASK_BIAS)
    per_dr = jnp.moveaxis(per_dr, 1, 0)
    two = jnp.concatenate([per_dr[:-1], per_dr[1:]], axis=-1)
    return two.reshape(2 * NA_WIN_ROWS - 2, NA_PAIRS, 2 * GRID_W, 2 * GRID_W)


def _even_back_kernel(q_ref, kp_ref, kc_ref, kn_ref, vp_ref, vc_ref, vn_ref, sgb_ref, ya_ref, ym_ref,
                      x_ref, u_ref, w_ref, out_ref, kwin, vwin, yb_scr):
    rbt = q_ref.shape[1]
    rb = rbt // GRID_W
    j = pl.program_id(1)
    rows = pl.num_programs(1) * rb
    win_keys = NA_WIN_ROWS * GRID_W

    kwin[0:rbt] = kp_ref[0]
    kwin[rbt:2 * rbt] = kc_ref[0]
    kwin[2 * rbt:3 * rbt] = kn_ref[0]
    vwin[0:rbt] = vp_ref[0]
    vwin[rbt:2 * rbt] = vc_ref[0]
    vwin[2 * rbt:3 * rbt] = vn_ref[0]

    lane = lax.broadcasted_iota(jnp.int32, (GRID_W, 2 * NA_HEAD_DIM), 1)
    first_head = lane < NA_HEAD_DIM

    def row_body(i, carry):
        r = j * rb + i
        rstart = jnp.clip(r - NA_WIN_ROWS // 2, 0, rows - NA_WIN_ROWS)
        local = pl.multiple_of((rstart - (j - 1) * rb) * GRID_W, GRID_W)
        off = rstart - r + NA_WIN_ROWS - 1
        qoff = pl.multiple_of(i * GRID_W, GRID_W)
        for p in range(NA_PAIRS):
            lo, hi = p * 2 * NA_HEAD_DIM, (p + 1) * 2 * NA_HEAD_DIM
            qp = q_ref[0, pl.ds(qoff, GRID_W), lo:hi]
            zero = jnp.zeros_like(qp)
            q2 = jnp.concatenate([jnp.where(first_head, qp, zero), jnp.where(first_head, zero, qp)], axis=0)
            s = _dot_nt(q2, kwin[pl.ds(local, win_keys), lo:hi])
            bias = jnp.concatenate([u_ref[off + 2 * m, p] for m in range(NA_WIN_ROWS // 2)], axis=1)
            s = s + bias
            e = jnp.exp(s - jnp.max(s, axis=-1, keepdims=True))
            o2 = _dot(e.astype(BF16), vwin[pl.ds(local, win_keys), lo:hi])
            o2 = o2 / jnp.sum(e, axis=-1, keepdims=True)
            o = jnp.where(first_head, o2[:GRID_W], o2[GRID_W:])
            gate = sgb_ref[0, pl.ds(qoff, GRID_W), lo:hi].astype(F32)
            yb_scr[pl.ds(qoff, GRID_W), lo:hi] = (o * gate).astype(BF16)
        return carry

    lax.fori_loop(0, rb, row_body, 0)

    acc = x_ref[0]
    acc = acc + _dot(ya_ref[0], w_ref[0:D_MODEL, :])
    acc = acc + _dot(yb_scr[...], w_ref[D_MODEL:2 * D_MODEL, :])
    acc = acc + _dot(ym_ref[0], w_ref[2 * D_MODEL:2 * D_MODEL + MEM_W, :])
    out_ref[0] = acc


def _even_back(x, q, k, v, sgb, ya, ym, ubias, w_out):
    bsz, t_len, _ = x.shape
    rbt = NA_ROW_BLOCK * GRID_W
    nblk = t_len // rbt
    cur = lambda width: pl.BlockSpec((1, rbt, width), lambda b, j: (b, j, 0))
    prev = pl.BlockSpec((1, rbt, D_MODEL), lambda b, j: (b, jnp.maximum(j - 1, 0), 0))
    nxt = pl.BlockSpec((1, rbt, D_MODEL), lambda b, j: (b, jnp.minimum(j + 1, nblk - 1), 0))
    return pl.pallas_call(
        _even_back_kernel,
        grid=(bsz, nblk),
        in_specs=[cur(D_MODEL), prev, cur(D_MODEL), nxt, prev, cur(D_MODEL), nxt,
                  cur(D_MODEL), cur(D_MODEL), cur(MEM_W), cur(D_MODEL),
                  _const_spec(ubias.shape), _const_spec(w_out.shape)],
        out_specs=cur(D_MODEL),
        out_shape=jax.ShapeDtypeStruct((bsz, t_len, D_MODEL), F32),
        scratch_shapes=[pltpu.VMEM((3 * rbt, D_MODEL), BF16), pltpu.VMEM((3 * rbt, D_MODEL), BF16),
                        pltpu.VMEM((rbt, D_MODEL), BF16)],
        compiler_params=pltpu.CompilerParams(
            dimension_semantics=("arbitrary", "arbitrary"), vmem_limit_bytes=V7X_VMEM_LIMIT_BYTES),
        name="even_back",
    )(q, k, k, k, v, v, v, sgb, ya, ym, x, ubias, w_out)


def _lru_coeffs(xc, wa_ref, wi_ref, ba, bi, sp, n):
    xb = xc.astype(BF16)
    r = _sigmoid(_dot(xb, wa_ref[n]) + ba)
    gate_i = _sigmoid(_dot(xb, wi_ref[n]) + bi)
    a = jnp.exp(-LRU_C * r * sp)
    return a, jnp.sqrt(1.0 - a * a) * (gate_i * xc)


def _softplus_neg(lam):
    return jnp.maximum(-lam, 0.0) + jnp.log(1.0 + jnp.exp(-jnp.abs(lam)))


def _scan_in_groups(a, b, reverse):
    tt = a.shape[0]
    pos = lax.broadcasted_iota(jnp.int32, (tt, 1), 0) & (SUBLANES - 1)
    k = 1
    while k < SUBLANES:
        if reverse:
            ok = pos < SUBLANES - k
            shift = tt - k
        else:
            ok = pos >= k
            shift = k
        a_s = jnp.where(ok, pltpu.roll(a, shift, 0), 1.0)
        b_s = jnp.where(ok, pltpu.roll(b, shift, 0), 0.0)
        b = a * b_s + b
        a = a * a_s
        k *= 2
    return a, b


def _scan_across_groups(a_scr, b_scr, carry_scr, write, reverse):
    tt = a_scr.shape[0]
    ngroups = tt // SUBLANES
    edge = 0 if reverse else SUBLANES - 1

    def body(step, carry):
        g = (ngroups - 1 - step) if reverse else step
        start = pl.multiple_of(g * SUBLANES, SUBLANES)
        hblk = a_scr[pl.ds(start, SUBLANES), :] * carry + b_scr[pl.ds(start, SUBLANES), :]
        write(start, hblk)
        return jnp.broadcast_to(hblk[edge:edge + 1, :], hblk.shape)

    carry_scr[...] = lax.fori_loop(0, ngroups, body, carry_scr[...])


def _odd_front_kernel(x_ref, xp_ref, xn_ref, g_ref, w_ref, cw_ref, cb_ref, wa_ref, wi_ref, ba_ref, bi_ref,
                      lam_ref, sg_ref, sw_ref, sbt_ref, kv_ref,
                      xc_ref, hf_ref, sgc_ref, yd_ref, ym_ref, xr_scr, a_scr, b_scr, carry_scr):
    tt = x_ref.shape[1]
    h, h_ext = _normed_with_halo(x_ref, xp_ref, xn_ref, g_ref[...])

    @pl.when(pl.program_id(1) == 0)
    def _():
        carry_scr[...] = jnp.zeros_like(carry_scr)

    xr_scr[...] = jnp.where(_halo_valid(tt), _dot(h_ext, w_ref[:, OD_XR:OD_XR + D_MODEL]), 0.0)
    xc = cb_ref[...]
    for j in range(LRU_CONV_K):
        xc = xc + cw_ref[j:j + 1, :] * xr_scr[pl.ds(HALO - LRU_CONV_K // 2 + j, tt), :]
    xc_ref[0] = xc
    sp = _softplus_neg(lam_ref[...])
    for n in range(LRU_BLOCKS):
        lo, hi = n * LRU_BW, (n + 1) * LRU_BW
        a, b = _lru_coeffs(xc[:, lo:hi], wa_ref, wi_ref, ba_ref[:, lo:hi], bi_ref[:, lo:hi], sp[:, lo:hi], n)
        a, b = _scan_in_groups(a, b, reverse=False)
        a_scr[:, lo:hi] = a
        b_scr[:, lo:hi] = b

    def write(start, hblk):
        hf_ref[0, pl.ds(start, SUBLANES), :] = hblk

    _scan_across_groups(a_scr, b_scr, carry_scr, write, reverse=False)
    sgc_ref[0] = _silu(_dot(h, w_ref[:, OD_GC:OD_GC + D_MODEL])).astype(BF16)

    u = _dot(h, w_ref[:, OD_U:OD_U + D_MODEL])
    vn = _rms(_dot(h, w_ref[:, OD_V:OD_V + D_MODEL]), sg_ref[...]).astype(BF16)
    sgd = _silu(_dot(h, w_ref[:, OD_GD:OD_GD + D_MODEL]))
    for c in range(tt // SGU_CHUNK):
        r0, r1 = c * SGU_CHUNK, (c + 1) * SGU_CHUNK
        for grp in range(SGU_GROUPS):
            lo, hi = grp * LRU_BW, (grp + 1) * LRU_BW
            mixed = _dot(sw_ref[grp], vn[r0:r1, lo:hi]) + sbt_ref[:, grp:grp + 1]
            yd_ref[0, r0:r1, lo:hi] = (u[r0:r1, lo:hi] * mixed * sgd[r0:r1, lo:hi]).astype(BF16)

    qm = _dot(h, w_ref[:, OD_QM:OD_QM + MEM_W])
    gm = _dot(h, w_ref[:, OD_GM:OD_GM + MEM_W])
    _mem_attention(qm, gm, kv_ref, ym_ref)


def _odd_front(x, g, w_in, conv_w, conv_b, wa, wi, ba, bi, lam, sgu_g, sgu_w, sgu_bt, kv):
    bsz, t_len, _ = x.shape
    tt = min(TOKEN_TILE, t_len)
    act = lambda width: pl.BlockSpec((1, tt, width), lambda b, i: (b, i, 0))
    sd = lambda width, dt: jax.ShapeDtypeStruct((bsz, t_len, width), dt)
    row = _const_spec((1, D_MODEL))
    return pl.pallas_call(
        _odd_front_kernel,
        grid=(bsz, t_len // tt),
        in_specs=_tile_specs(tt, t_len) + [
            row, _const_spec(w_in.shape), _const_spec(conv_w.shape), row,
            _const_spec(wa.shape), _const_spec(wi.shape), row, row, row,
            row, _const_spec(sgu_w.shape), _const_spec(sgu_bt.shape),
            pl.BlockSpec((1, N_MEM, 2 * MEM_W), lambda b, i: (b, 0, 0)),
        ],
        out_specs=[act(D_MODEL)] * 4 + [act(MEM_W)],
        out_shape=[sd(D_MODEL, F32), sd(D_MODEL, F32), sd(D_MODEL, BF16), sd(D_MODEL, BF16), sd(MEM_W, BF16)],
        scratch_shapes=[pltpu.VMEM((tt + 2 * HALO, D_MODEL), F32), pltpu.VMEM((tt, D_MODEL), F32),
                        pltpu.VMEM((tt, D_MODEL), F32), pltpu.VMEM((SUBLANES, D_MODEL), F32)],
        compiler_params=pltpu.CompilerParams(
            dimension_semantics=("arbitrary", "arbitrary"), vmem_limit_bytes=V7X_VMEM_LIMIT_BYTES),
        name="odd_front",
    )(x, x, x, g.reshape(1, D_MODEL), w_in, conv_w, conv_b.reshape(1, D_MODEL), wa, wi,
      ba.reshape(1, D_MODEL), bi.reshape(1, D_MODEL), lam.reshape(1, D_MODEL),
      sgu_g.reshape(1, D_MODEL), sgu_w, sgu_bt, kv)


def _odd_back_kernel(xc_ref, hf_ref, sgc_ref, yd_ref, ym_ref, x_ref, wa_ref, wi_ref, ba_ref, bi_ref, lam_ref,
                     w_ref, fg_ref, out_ref, a_scr, b_scr, carry_scr, yc_scr):
    @pl.when(pl.program_id(1) == 0)
    def _():
        carry_scr[...] = jnp.zeros_like(carry_scr)

    sp = _softplus_neg(lam_ref[...])
    for n in range(LRU_BLOCKS):
        lo, hi = n * LRU_BW, (n + 1) * LRU_BW
        a, b = _lru_coeffs(xc_ref[0, :, lo:hi], wa_ref, wi_ref, ba_ref[:, lo:hi], bi_ref[:, lo:hi],
                           sp[:, lo:hi], n)
        a, b = _scan_in_groups(a, b, reverse=True)
        a_scr[:, lo:hi] = a
        b_scr[:, lo:hi] = b

    def write(start, hblk):
        rows = pl.ds(start, SUBLANES)
        yc_scr[rows, :] = (hf_ref[0, rows, :] + hblk) * sgc_ref[0, rows, :].astype(F32)

    _scan_across_groups(a_scr, b_scr, carry_scr, write, reverse=True)

    acc = x_ref[0]
    acc = acc + _dot(yc_scr[...].astype(BF16), w_ref[0:D_MODEL, :])
    acc = acc + _dot(yd_ref[0], w_ref[D_MODEL:2 * D_MODEL, :])
    acc = acc + _dot(ym_ref[0], w_ref[2 * D_MODEL:2 * D_MODEL + MEM_W, :])
    out_ref[0] = _rms(acc, fg_ref[...])


def _odd_back(x, xc, hf, sgc, yd, ym, wa, wi, ba, bi, lam, w_out, final_g):
    bsz, t_len, _ = x.shape
    tt = min(TOKEN_TILE, t_len)
    nt = t_len // tt
    act = lambda width: pl.BlockSpec((1, tt, width), lambda b, i: (b, nt - 1 - i, 0))
    row = _const_spec((1, D_MODEL))
    return pl.pallas_call(
        _odd_back_kernel,
        grid=(bsz, nt),
        in_specs=[act(D_MODEL)] * 4 + [act(MEM_W), act(D_MODEL),
                  _const_spec(wa.shape), _const_spec(wi.shape), row, row, row,
                  _const_spec(w_out.shape), row],
        out_specs=act(D_MODEL),
        out_shape=jax.ShapeDtypeStruct((bsz, t_len, D_MODEL), F32),
        scratch_shapes=[pltpu.VMEM((tt, D_MODEL), F32), pltpu.VMEM((tt, D_MODEL), F32),
                        pltpu.VMEM((SUBLANES, D_MODEL), F32), pltpu.VMEM((tt, D_MODEL), F32)],
        compiler_params=pltpu.CompilerParams(
            dimension_semantics=("arbitrary", "arbitrary"), vmem_limit_bytes=V7X_VMEM_LIMIT_BYTES),
        name="odd_back",
    )(xc, hf, sgc, yd, ym, x, wa, wi, ba.reshape(1, D_MODEL), bi.reshape(1, D_MODEL),
      lam.reshape(1, D_MODEL), w_out, final_g.reshape(1, D_MODEL))


def _trunk(x, kv, norm_g, ev_w_in, ev_conv_w, ev_conv_b, ubias, ev_w_out,
           od_w_in, od_conv_w, od_conv_b, od_lru_wa, od_lru_ba, od_lru_wi, od_lru_bi, od_lru_lam,
           od_sgu_g, od_sgu_w, od_sgu_bt, od_w_out, final_g):
    ya, q, k, v, sgb, ym = _even_front(x, norm_g[0], ev_w_in, ev_conv_w, ev_conv_b, kv[0])
    x = _even_back(x, q, k, v, sgb, ya, ym, ubias, ev_w_out)
    xc, hf, sgc, yd, ym = _odd_front(x, norm_g[1], od_w_in, od_conv_w, od_conv_b,
                                     od_lru_wa[0], od_lru_wi[0], od_lru_ba[0], od_lru_bi[0], od_lru_lam[0],
                                     od_sgu_g, od_sgu_w, od_sgu_bt, kv[1])
    return _odd_back(x, xc, hf, sgc, yd, ym, od_lru_wa[1], od_lru_wi[1], od_lru_ba[1], od_lru_bi[1],
                     od_lru_lam[1], od_w_out, final_g)


def kernel(x_prompt, x_sample, mem_prompt, mem_sample, norm_g, mem_norm_g, w_mem_kv, ev_w_in, ev_conv_w,
           ev_conv_b, ev_rpb, ev_w_out, od_w_in, od_conv_w, od_conv_b, od_lru_wa, od_lru_ba, od_lru_wi,
           od_lru_bi, od_lru_lam, od_sgu_g, od_sgu_w, od_sgu_b, od_w_out, final_g):
    depth = norm_g.shape[0]
    assert depth == 2 and ev_w_in.shape[0] == 1 and od_w_in.shape[0] == 1
    nb_p, nb_s = mem_prompt.shape[0], mem_sample.shape[0]
    mem_all = jnp.concatenate([mem_prompt.reshape(-1, D_MODEL), mem_sample.reshape(-1, D_MODEL)], axis=0)
    kv = _memkv(mem_all, mem_norm_g, w_mem_kv.astype(BF16)).reshape(depth, nb_p + nb_s, N_MEM, 2 * MEM_W)
    shared = (norm_g, ev_w_in[0].astype(BF16), ev_conv_w[0], ev_conv_b[0], _na_bias_table(ev_rpb[0]),
              ev_w_out[0].astype(BF16), od_w_in[0].astype(BF16), od_conv_w[0], od_conv_b[0],
              od_lru_wa[0].astype(BF16), od_lru_ba[0], od_lru_wi[0].astype(BF16), od_lru_bi[0], od_lru_lam[0],
              od_sgu_g[0], od_sgu_w[0].astype(BF16), od_sgu_b[0].T, od_w_out[0].astype(BF16), final_g)
    y_prompt = _trunk(x_prompt, kv[:, :nb_p], *shared)
    y_sample = _trunk(x_sample, kv[:, nb_p:], *shared)
    return (y_prompt, y_sample)
```

```python
import functools

import jax
import jax.numpy as jnp
from jax import lax
from jax.experimental import pallas as pl
from jax.experimental.pallas import tpu as pltpu

F32 = jnp.float32
BF16 = jnp.bfloat16

D_MODEL = 1024
GRID_W = 64
N_MEM = 256
EPS = 1e-6
CONV_K = 3
NA_HEADS = 16
NA_HEAD_DIM = 64
NA_WIN_ROWS = 8
NA_WIN_COLS = 16
LRU_BLOCKS = 8
LRU_BW = 128
LRU_CONV_K = 4
LRU_C = 8.0
SGU_GROUPS = 8
SGU_CHUNK = 128
MEM_HEADS = 4
MEM_HEAD_DIM = 128
MEM_W = MEM_HEADS * MEM_HEAD_DIM

EV_BG, EV_CG, EV_XV, EV_GA, EV_Q, EV_K, EV_V, EV_GB, EV_QM, EV_GM = (
    0, 1024, 2048, 3072, 4096, 5120, 6144, 7168, 8192, 8704)
OD_XR, OD_GC, OD_U, OD_V, OD_GD, OD_QM, OD_GM = (0, 1024, 2048, 3072, 4096, 5120, 5632)

V7X_VMEM_LIMIT_BYTES = 56 * 1024 * 1024
SUBLANES = 8
MXU_DEPTH = 256
HALO = 16
TOKEN_TILE = 512
NA_ROW_BLOCK = 8
NA_PAIRS = NA_HEADS // 2
MASK_BIAS = -1e30


def _dot(a, b):
    return jnp.dot(a, b, preferred_element_type=F32)


def _dot_nt(a, b):
    return lax.dot_general(a, b, (((1,), (1,)), ((), ())), preferred_element_type=F32)


def _rms(x, g):
    return x * lax.rsqrt(jnp.mean(x * x, axis=-1, keepdims=True) + EPS) * g


def _sigmoid(x):
    return 1.0 / (1.0 + jnp.exp(-x))


def _silu(x):
    return x * _sigmoid(x)


def _mem_attention(qm, gm, kv_ref, out_ref):
    scale = MEM_HEAD_DIM ** -0.5
    for hh in range(MEM_HEADS):
        lo, hi = hh * MEM_HEAD_DIM, (hh + 1) * MEM_HEAD_DIM
        s = _dot_nt(qm[:, lo:hi].astype(BF16), kv_ref[0, :, lo:hi]) * scale
        e = jnp.exp(s - jnp.max(s, axis=-1, keepdims=True))
        o = _dot(e.astype(BF16), kv_ref[0, :, MEM_W + lo:MEM_W + hi])
        o = o / jnp.sum(e, axis=-1, keepdims=True)
        out_ref[0, :, lo:hi] = (o * _silu(gm[:, lo:hi])).astype(out_ref.dtype)


def _normed_with_halo(x_ref, xp_ref, xn_ref, g):
    h = _rms(x_ref[0], g).astype(BF16)
    hp = _rms(xp_ref[0], g).astype(BF16)
    hn = _rms(xn_ref[0], g).astype(BF16)
    return h, jnp.concatenate([hp, h, hn], axis=0)


def _halo_valid(tt):
    i = pl.program_id(1)
    last = pl.num_programs(1) - 1
    row = lax.broadcasted_iota(jnp.int32, (tt + 2 * HALO, 1), 0)
    return ((row >= HALO) | (i > 0)) & ((row < tt + HALO) | (i < last))


def _memkv_kernel(mem_ref, g_ref, w_ref, kv_ref):
    hn = _rms(mem_ref[...], g_ref[0]).astype(BF16)
    kv_ref[0] = _dot(hn, w_ref[0]).astype(BF16)


def _memkv(mem_all, mem_norm_g, w_mem_kv):
    depth = w_mem_kv.shape[0]
    nb = mem_all.shape[0] // N_MEM
    return pl.pallas_call(
        _memkv_kernel,
        grid=(depth, nb),
        in_specs=[
            pl.BlockSpec((N_MEM, D_MODEL), lambda l, b: (b, 0)),
            pl.BlockSpec((1, 1, D_MODEL), lambda l, b: (l, 0, 0)),
            pl.BlockSpec((1, D_MODEL, 2 * MEM_W), lambda l, b: (l, 0, 0)),
        ],
        out_specs=pl.BlockSpec((1, N_MEM, 2 * MEM_W), lambda l, b: (l, b, 0)),
        out_shape=jax.ShapeDtypeStruct((depth, nb * N_MEM, 2 * MEM_W), BF16),
        compiler_params=pltpu.CompilerParams(
            dimension_semantics=("arbitrary", "arbitrary"), vmem_limit_bytes=V7X_VMEM_LIMIT_BYTES),
        name="memkv",
    )(mem_all, mem_norm_g.reshape(depth, 1, D_MODEL), w_mem_kv)


def _even_front_kernel(x_ref, xp_ref, xn_ref, g_ref, w_ref, cw_ref, cb_ref, kv_ref,
                       ya_ref, q_ref, k_ref, v_ref, sgb_ref, ym_ref, p_scr):
    tt = x_ref.shape[1]
    h, h_ext = _normed_with_halo(x_ref, xp_ref, xn_ref, g_ref[...])

    cg = _dot(h_ext, w_ref[:, EV_CG:EV_CG + D_MODEL])
    xv = _dot(h_ext, w_ref[:, EV_XV:EV_XV + D_MODEL])
    p_scr[...] = jnp.where(_halo_valid(tt), cg * xv, 0.0)
    conv = cb_ref[...]
    for j in range(CONV_K):
        conv = conv + cw_ref[j:j + 1, :] * p_scr[pl.ds(HALO - CONV_K // 2 + j, tt), :]
    bg = _dot(h, w_ref[:, EV_BG:EV_BG + D_MODEL])
    ga = _dot(h, w_ref[:, EV_GA:EV_GA + D_MODEL])
    ya_ref[0] = (bg * conv * _silu(ga)).astype(BF16)

    q_ref[0] = (_dot(h, w_ref[:, EV_Q:EV_Q + D_MODEL]) * (NA_HEAD_DIM ** -0.5)).astype(BF16)
    k_ref[0] = _dot(h, w_ref[:, EV_K:EV_K + D_MODEL]).astype(BF16)
    v_ref[0] = _dot(h, w_ref[:, EV_V:EV_V + D_MODEL]).astype(BF16)
    sgb_ref[0] = _silu(_dot(h, w_ref[:, EV_GB:EV_GB + D_MODEL])).astype(BF16)

    qm = _dot(h, w_ref[:, EV_QM:EV_QM + MEM_W])
    gm = _dot(h, w_ref[:, EV_GM:EV_GM + MEM_W])
    _mem_attention(qm, gm, kv_ref, ym_ref)


def _tile_specs(tt, t_len):
    per = tt // HALO
    n_halo = t_len // HALO
    return [
        pl.BlockSpec((1, tt, D_MODEL), lambda b, i: (b, i, 0)),
        pl.BlockSpec((1, HALO, D_MODEL), lambda b, i: (b, jnp.maximum(i * per - 1, 0), 0)),
        pl.BlockSpec((1, HALO, D_MODEL), lambda b, i: (b, jnp.minimum((i + 1) * per, n_halo - 1), 0)),
    ]


def _const_spec(shape):
    nd = len(shape)
    return pl.BlockSpec(shape, lambda b, i: (0,) * nd, pipeline_mode=pl.Buffered(1))


def _even_front(x, g, w_in, conv_w, conv_b, kv):
    bsz, t_len, _ = x.shape
    tt = min(TOKEN_TILE, t_len)
    act = lambda width: pl.BlockSpec((1, tt, width), lambda b, i: (b, i, 0))
    out_sd = lambda width: jax.ShapeDtypeStruct((bsz, t_len, width), BF16)
    return pl.pallas_call(
        _even_front_kernel,
        grid=(bsz, t_len // tt),
        in_specs=_tile_specs(tt, t_len) + [
            _const_spec((1, D_MODEL)),
            _const_spec(w_in.shape),
            _const_spec(conv_w.shape),
            _const_spec((1, D_MODEL)),
            pl.BlockSpec((1, N_MEM, 2 * MEM_W), lambda b, i: (b, 0, 0)),
        ],
        out_specs=[act(D_MODEL)] * 5 + [act(MEM_W)],
        out_shape=[out_sd(D_MODEL)] * 5 + [out_sd(MEM_W)],
        scratch_shapes=[pltpu.VMEM((tt + 2 * HALO, D_MODEL), F32)],
        compiler_params=pltpu.CompilerParams(
            dimension_semantics=("arbitrary", "arbitrary"), vmem_limit_bytes=V7X_VMEM_LIMIT_BYTES),
        name="even_front",
    )(x, x, x, g.reshape(1, D_MODEL), w_in, conv_w, conv_b.reshape(1, D_MODEL), kv)


def _na_bias_table(rpb):
    col = jnp.arange(GRID_W)
    cstart = jnp.clip(col - NA_WIN_COLS // 2, 0, GRID_W - NA_WIN_COLS)
    col_ok = (col[None, :] >= cstart[:, None]) & (col[None, :] < cstart[:, None] + NA_WIN_COLS)
    dc_idx = jnp.clip(col[None, :] - col[:, None], -(NA_WIN_COLS - 1), NA_WIN_COLS - 1) + NA_WIN_COLS - 1
    per_dr = jnp.where(col_ok[None, None], rpb.astype(F32)[:, :, dc_idx], MASK_BIAS)
    per_dr = jnp.moveaxis(per_dr, 1, 0)
    two = jnp.concatenate([per_dr[:-1], per_dr[1:]], axis=-1)
    return two.reshape(2 * NA_WIN_ROWS - 2, NA_PAIRS, 2 * GRID_W, 2 * GRID_W)


def _even_back_kernel(q_ref, kp_ref, kc_ref, kn_ref, vp_ref, vc_ref, vn_ref, sgb_ref, ya_ref, ym_ref,
                      x_ref, u_ref, w_ref, out_ref, kwin, vwin, yb_scr, s_scr, m_scr):
    rbt = q_ref.shape[1]
    rb = rbt // GRID_W
    j = pl.program_id(1)
    rows = pl.num_programs(1) * rb
    win_keys = NA_WIN_ROWS * GRID_W

    kwin[0:rbt] = kp_ref[0]
    kwin[rbt:2 * rbt] = kc_ref[0]
    kwin[2 * rbt:3 * rbt] = kn_ref[0]
    vwin[0:rbt] = vp_ref[0]
    vwin[rbt:2 * rbt] = vc_ref[0]
    vwin[2 * rbt:3 * rbt] = vn_ref[0]

    lane = lax.broadcasted_iota(jnp.int32, (GRID_W, 2 * NA_HEAD_DIM), 1)
    first_head = lane < NA_HEAD_DIM

    def row_body(i, carry):
        r = j * rb + i
        rstart = jnp.clip(r - NA_WIN_ROWS // 2, 0, rows - NA_WIN_ROWS)
        local = pl.multiple_of((rstart - (j - 1) * rb) * GRID_W, GRID_W)
        off = rstart - r + NA_WIN_ROWS - 1
        qoff = pl.multiple_of(i * GRID_W, GRID_W)
        for p in range(NA_PAIRS):
            lo, hi = p * 2 * NA_HEAD_DIM, (p + 1) * 2 * NA_HEAD_DIM
            qp = q_ref[0, pl.ds(qoff, GRID_W), lo:hi]
            zero = jnp.zeros_like(qp)
            q2 = jnp.concatenate([jnp.where(first_head, qp, zero), jnp.where(first_head, zero, qp)], axis=0)
            s = _dot_nt(q2, kwin[pl.ds(local, win_keys), lo:hi])
            bias = jnp.concatenate([u_ref[off + 2 * m, p] for m in range(NA_WIN_ROWS // 2)], axis=1)
            s = s + bias
            s_scr[p] = s
            m_scr[p] = jnp.broadcast_to(jnp.max(s, axis=-1, keepdims=True), (2 * GRID_W, 2 * NA_HEAD_DIM))
        for p in range(NA_PAIRS):
            lo, hi = p * 2 * NA_HEAD_DIM, (p + 1) * 2 * NA_HEAD_DIM
            m = m_scr[p]
            e = jnp.concatenate(
                [jnp.exp(s_scr[p, :, c * 128:(c + 1) * 128] - m) for c in range(win_keys // 128)], axis=1)
            o2 = _dot(e.astype(BF16), vwin[pl.ds(local, win_keys), lo:hi])
            o2 = o2 / jnp.sum(e, axis=-1, keepdims=True)
            o = jnp.where(first_head, o2[:GRID_W], o2[GRID_W:])
            gate = sgb_ref[0, pl.ds(qoff, GRID_W), lo:hi].astype(F32)
            yb_scr[pl.ds(qoff, GRID_W), lo:hi] = (o * gate).astype(BF16)
        return carry

    lax.fori_loop(0, rb, row_body, 0)

    acc = x_ref[0]
    acc = acc + _dot(ya_ref[0], w_ref[0:D_MODEL, :])
    acc = acc + _dot(yb_scr[...], w_ref[D_MODEL:2 * D_MODEL, :])
    acc = acc + _dot(ym_ref[0], w_ref[2 * D_MODEL:2 * D_MODEL + MEM_W, :])
    out_ref[0] = acc


def _even_back(x, q, k, v, sgb, ya, ym, ubias, w_out):
    bsz, t_len, _ = x.shape
    rbt = NA_ROW_BLOCK * GRID_W
    nblk = t_len // rbt
    cur = lambda width: pl.BlockSpec((1, rbt, width), lambda b, j: (b, j, 0))
    prev = pl.BlockSpec((1, rbt, D_MODEL), lambda b, j: (b, jnp.maximum(j - 1, 0), 0))
    nxt = pl.BlockSpec((1, rbt, D_MODEL), lambda b, j: (b, jnp.minimum(j + 1, nblk - 1), 0))
    return pl.pallas_call(
        _even_back_kernel,
        grid=(bsz, nblk),
        in_specs=[cur(D_MODEL), prev, cur(D_MODEL), nxt, prev, cur(D_MODEL), nxt,
                  cur(D_MODEL), cur(D_MODEL), cur(MEM_W), cur(D_MODEL),
                  _const_spec(ubias.shape), _const_spec(w_out.shape)],
        out_specs=cur(D_MODEL),
        out_shape=jax.ShapeDtypeStruct((bsz, t_len, D_MODEL), F32),
        scratch_shapes=[pltpu.VMEM((3 * rbt, D_MODEL), BF16), pltpu.VMEM((3 * rbt, D_MODEL), BF16),
                        pltpu.VMEM((rbt, D_MODEL), BF16),
                        pltpu.VMEM((NA_PAIRS, 2 * GRID_W, NA_WIN_ROWS * GRID_W), F32),
                        pltpu.VMEM((NA_PAIRS, 2 * GRID_W, 2 * NA_HEAD_DIM), F32)],
        compiler_params=pltpu.CompilerParams(
            dimension_semantics=("arbitrary", "arbitrary"), vmem_limit_bytes=V7X_VMEM_LIMIT_BYTES),
        name="even_back",
    )(q, k, k, k, v, v, v, sgb, ya, ym, x, ubias, w_out)


def _lru_coeffs(xc, wa_ref, wi_ref, ba, bi, sp, n):
    xb = xc.astype(BF16)
    r = _sigmoid(_dot(xb, wa_ref[n]) + ba)
    gate_i = _sigmoid(_dot(xb, wi_ref[n]) + bi)
    a = jnp.exp(-LRU_C * r * sp)
    return a, jnp.sqrt(1.0 - a * a) * (gate_i * xc)


def _softplus_neg(lam):
    return jnp.maximum(-lam, 0.0) + jnp.log(1.0 + jnp.exp(-jnp.abs(lam)))


def _lru_scan(a, b, carry, reverse):
    tt, width = a.shape
    ngroups = tt // SUBLANES
    a = a.reshape(ngroups, SUBLANES, width)
    b = b.reshape(ngroups, SUBLANES, width)
    pos = lax.broadcasted_iota(jnp.int32, (1, SUBLANES, width), 1)
    k = 1
    while k < SUBLANES:
        ok = (pos < SUBLANES - k) if reverse else (pos >= k)
        shift = SUBLANES - k if reverse else k
        a_s = jnp.where(ok, pltpu.roll(a, shift, 1), 1.0)
        b_s = jnp.where(ok, pltpu.roll(b, shift, 1), 0.0)
        b = a * b_s + b
        a = a * a_s
        k *= 2
    edge = 0 if reverse else SUBLANES - 1
    hs = [None] * ngroups
    for step in range(ngroups):
        g = (ngroups - 1 - step) if reverse else step
        hs[g] = a[g] * carry + b[g]
        carry = hs[g][edge:edge + 1, :]
    return jnp.concatenate(hs, axis=0), carry


def _odd_front_kernel(x_ref, xp_ref, xn_ref, g_ref, w_ref, cw_ref, cb_ref, wa_ref, wi_ref, ba_ref, bi_ref,
                      lam_ref, sg_ref, sw_ref, sbt_ref, kv_ref,
                      xc_ref, hf_ref, sgc_ref, yd_ref, ym_ref, xr_scr, carry_scr):
    tt = x_ref.shape[1]
    h, h_ext = _normed_with_halo(x_ref, xp_ref, xn_ref, g_ref[...])

    @pl.when(pl.program_id(1) == 0)
    def _():
        carry_scr[...] = jnp.zeros_like(carry_scr)

    xr_scr[...] = jnp.where(_halo_valid(tt), _dot(h_ext, w_ref[:, OD_XR:OD_XR + D_MODEL]), 0.0)
    xc = cb_ref[...]
    for j in range(LRU_CONV_K):
        xc = xc + cw_ref[j:j + 1, :] * xr_scr[pl.ds(HALO - LRU_CONV_K // 2 + j, tt), :]
    xc_ref[0] = xc
    sp = _softplus_neg(lam_ref[...])
    for n in range(LRU_BLOCKS):
        lo, hi = n * LRU_BW, (n + 1) * LRU_BW
        a, b = _lru_coeffs(xc[:, lo:hi], wa_ref, wi_ref, ba_ref[:, lo:hi], bi_ref[:, lo:hi], sp[:, lo:hi], n)
        hf_ref[0, :, lo:hi], carry_scr[:, lo:hi] = _lru_scan(a, b, carry_scr[:, lo:hi], reverse=False)
    sgc_ref[0] = _silu(_dot(h, w_ref[:, OD_GC:OD_GC + D_MODEL])).astype(BF16)

    u = _dot(h, w_ref[:, OD_U:OD_U + D_MODEL])
    vn = _rms(_dot(h, w_ref[:, OD_V:OD_V + D_MODEL]), sg_ref[...]).astype(BF16)
    sgd = _silu(_dot(h, w_ref[:, OD_GD:OD_GD + D_MODEL]))
    for c in range(tt // SGU_CHUNK):
        r0, r1 = c * SGU_CHUNK, (c + 1) * SGU_CHUNK
        for grp in range(SGU_GROUPS):
            lo, hi = grp * LRU_BW, (grp + 1) * LRU_BW
            mixed = _dot(sw_ref[grp], vn[r0:r1, lo:hi]) + sbt_ref[:, grp:grp + 1]
            yd_ref[0, r0:r1, lo:hi] = (u[r0:r1, lo:hi] * mixed * sgd[r0:r1, lo:hi]).astype(BF16)

    qm = _dot(h, w_ref[:, OD_QM:OD_QM + MEM_W])
    gm = _dot(h, w_ref[:, OD_GM:OD_GM + MEM_W])
    _mem_attention(qm, gm, kv_ref, ym_ref)


def _odd_front(x, g, w_in, conv_w, conv_b, wa, wi, ba, bi, lam, sgu_g, sgu_w, sgu_bt, kv):
    bsz, t_len, _ = x.shape
    tt = min(TOKEN_TILE, t_len)
    act = lambda width: pl.BlockSpec((1, tt, width), lambda b, i: (b, i, 0))
    sd = lambda width, dt: jax.ShapeDtypeStruct((bsz, t_len, width), dt)
    row = _const_spec((1, D_MODEL))
    return pl.pallas_call(
        _odd_front_kernel,
        grid=(bsz, t_len // tt),
        in_specs=_tile_specs(tt, t_len) + [
            row, _const_spec(w_in.shape), _const_spec(conv_w.shape), row,
            _const_spec(wa.shape), _const_spec(wi.shape), row, row, row,
            row, _const_spec(sgu_w.shape), _const_spec(sgu_bt.shape),
            pl.BlockSpec((1, N_MEM, 2 * MEM_W), lambda b, i: (b, 0, 0)),
        ],
        out_specs=[act(D_MODEL)] * 4 + [act(MEM_W)],
        out_shape=[sd(D_MODEL, F32), sd(D_MODEL, F32), sd(D_MODEL, BF16), sd(D_MODEL, BF16), sd(MEM_W, BF16)],
        scratch_shapes=[pltpu.VMEM((tt + 2 * HALO, D_MODEL), F32), pltpu.VMEM((1, D_MODEL), F32)],
        compiler_params=pltpu.CompilerParams(
            dimension_semantics=("arbitrary", "arbitrary"), vmem_limit_bytes=V7X_VMEM_LIMIT_BYTES),
        name="odd_front",
    )(x, x, x, g.reshape(1, D_MODEL), w_in, conv_w, conv_b.reshape(1, D_MODEL), wa, wi,
      ba.reshape(1, D_MODEL), bi.reshape(1, D_MODEL), lam.reshape(1, D_MODEL),
      sgu_g.reshape(1, D_MODEL), sgu_w, sgu_bt, kv)


def _odd_back_kernel(xc_ref, hf_ref, sgc_ref, yd_ref, ym_ref, x_ref, wa_ref, wi_ref, ba_ref, bi_ref, lam_ref,
                     w_ref, fg_ref, out_ref, carry_scr):
    @pl.when(pl.program_id(1) == 0)
    def _():
        carry_scr[...] = jnp.zeros_like(carry_scr)

    independent = [(yd_ref, c, D_MODEL + c) for c in range(0, D_MODEL, MXU_DEPTH)]
    independent += [(ym_ref, c, 2 * D_MODEL + c) for c in range(0, MEM_W, MXU_DEPTH)]
    acc = x_ref[0]
    sp = _softplus_neg(lam_ref[...])
    yc = []
    for n in range(LRU_BLOCKS):
        lo, hi = n * LRU_BW, (n + 1) * LRU_BW
        a, b = _lru_coeffs(xc_ref[0, :, lo:hi], wa_ref, wi_ref, ba_ref[:, lo:hi], bi_ref[:, lo:hi],
                           sp[:, lo:hi], n)
        hb, carry_scr[:, lo:hi] = _lru_scan(a, b, carry_scr[:, lo:hi], reverse=True)
        yc.append(((hf_ref[0, :, lo:hi] + hb) * sgc_ref[0, :, lo:hi].astype(F32)).astype(BF16))
        if independent:
            ref, c, wrow = independent.pop(0)
            acc = acc + _dot(ref[0, :, c:c + MXU_DEPTH], w_ref[wrow:wrow + MXU_DEPTH, :])
        if n % 2 == 1:
            acc = acc + _dot(jnp.concatenate(yc, axis=1), w_ref[lo - LRU_BW:hi, :])
            yc = []
    assert not independent
    out_ref[0] = _rms(acc, fg_ref[...])


def _odd_back(x, xc, hf, sgc, yd, ym, wa, wi, ba, bi, lam, w_out, final_g):
    bsz, t_len, _ = x.shape
    tt = min(TOKEN_TILE, t_len)
    nt = t_len // tt
    act = lambda width: pl.BlockSpec((1, tt, width), lambda b, i: (b, nt - 1 - i, 0))
    row = _const_spec((1, D_MODEL))
    return pl.pallas_call(
        _odd_back_kernel,
        grid=(bsz, nt),
        in_specs=[act(D_MODEL)] * 4 + [act(MEM_W), act(D_MODEL),
                  _const_spec(wa.shape), _const_spec(wi.shape), row, row, row,
                  _const_spec(w_out.shape), row],
        out_specs=act(D_MODEL),
        out_shape=jax.ShapeDtypeStruct((bsz, t_len, D_MODEL), F32),
        scratch_shapes=[pltpu.VMEM((1, D_MODEL), F32)],
        compiler_params=pltpu.CompilerParams(
            dimension_semantics=("arbitrary", "arbitrary"), vmem_limit_bytes=V7X_VMEM_LIMIT_BYTES),
        name="odd_back",
    )(xc, hf, sgc, yd, ym, x, wa, wi, ba.reshape(1, D_MODEL), bi.reshape(1, D_MODEL),
      lam.reshape(1, D_MODEL), w_out, final_g.reshape(1, D_MODEL))


def _trunk(x, kv, norm_g, ev_w_in, ev_conv_w, ev_conv_b, ubias, ev_w_out,
           od_w_in, od_conv_w, od_conv_b, od_lru_wa, od_lru_ba, od_lru_wi, od_lru_bi, od_lru_lam,
           od_sgu_g, od_sgu_w, od_sgu_bt, od_w_out, final_g):
    ya, q, k, v, sgb, ym = _even_front(x, norm_g[0], ev_w_in, ev_conv_w, ev_conv_b, kv[0])
    x = _even_back(x, q, k, v, sgb, ya, ym, ubias, ev_w_out)
    xc, hf, sgc, yd, ym = _odd_front(x, norm_g[1], od_w_in, od_conv_w, od_conv_b,
                                     od_lru_wa[0], od_lru_wi[0], od_lru_ba[0], od_lru_bi[0], od_lru_lam[0],
                                     od_sgu_g, od_sgu_w, od_sgu_bt, kv[1])
    return _odd_back(x, xc, hf, sgc, yd, ym, od_lru_wa[1], od_lru_wi[1], od_lru_ba[1], od_lru_bi[1],
                     od_lru_lam[1], od_w_out, final_g)


def kernel(x_prompt, x_sample, mem_prompt, mem_sample, norm_g, mem_norm_g, w_mem_kv, ev_w_in, ev_conv_w,
           ev_conv_b, ev_rpb, ev_w_out, od_w_in, od_conv_w, od_conv_b, od_lru_wa, od_lru_ba, od_lru_wi,
           od_lru_bi, od_lru_lam, od_sgu_g, od_sgu_w, od_sgu_b, od_w_out, final_g):
    depth = norm_g.shape[0]
    assert depth == 2 and ev_w_in.shape[0] == 1 and od_w_in.shape[0] == 1
    nb_p, nb_s = mem_prompt.shape[0], mem_sample.shape[0]
    mem_all = jnp.concatenate([mem_prompt.reshape(-1, D_MODEL), mem_sample.reshape(-1, D_MODEL)], axis=0)
    kv = _memkv(mem_all, mem_norm_g, w_mem_kv.astype(BF16)).reshape(depth, nb_p + nb_s, N_MEM, 2 * MEM_W)
    shared = (norm_g, ev_w_in[0].astype(BF16), ev_conv_w[0], ev_conv_b[0], _na_bias_table(ev_rpb[0]),
              ev_w_out[0].astype(BF16), od_w_in[0].astype(BF16), od_conv_w[0], od_conv_b[0],
              od_lru_wa[0].astype(BF16), od_lru_ba[0], od_lru_wi[0].astype(BF16), od_lru_bi[0], od_lru_lam[0],
              od_sgu_g[0], od_sgu_w[0].astype(BF16), od_sgu_b[0].T, od_w_out[0].astype(BF16), final_g)
    y_prompt = _trunk(x_prompt, kv[:, :nb_p], *shared)
    y_sample = _trunk(x_sample, kv[:, nb_p:], *shared)
    return (y_prompt, y_sample)
```

```python
import functools

import jax
import jax.numpy as jnp
from jax import lax
from jax.experimental import pallas as pl
from jax.experimental.pallas import tpu as pltpu

F32 = jnp.float32
BF16 = jnp.bfloat16

D_MODEL = 1024
GRID_W = 64
N_MEM = 256
EPS = 1e-6
CONV_K = 3
NA_HEADS = 16
NA_HEAD_DIM = 64
NA_WIN_ROWS = 8
NA_WIN_COLS = 16
LRU_BLOCKS = 8
LRU_BW = 128
LRU_CONV_K = 4
LRU_C = 8.0
SGU_GROUPS = 8
SGU_CHUNK = 128
MEM_HEADS = 4
MEM_HEAD_DIM = 128
MEM_W = MEM_HEADS * MEM_HEAD_DIM

EV_BG, EV_CG, EV_XV, EV_GA, EV_Q, EV_K, EV_V, EV_GB, EV_QM, EV_GM = (
    0, 1024, 2048, 3072, 4096, 5120, 6144, 7168, 8192, 8704)
OD_XR, OD_GC, OD_U, OD_V, OD_GD, OD_QM, OD_GM = (0, 1024, 2048, 3072, 4096, 5120, 5632)

V7X_VMEM_LIMIT_BYTES = 60 * 1024 * 1024
SUBLANES = 8
MXU_DEPTH = 256
HALO = 16
TOKEN_TILE = 512
NA_ROW_BLOCK = 8
NA_PAIRS = NA_HEADS // 2
MASK_BIAS = -1e30
LOG2E = 1.4426950408889634


def _dot(a, b):
    return jnp.dot(a, b, preferred_element_type=F32)


def _dot_nt(a, b):
    return lax.dot_general(a, b, (((1,), (1,)), ((), ())), preferred_element_type=F32)


def _rms(x, g):
    return x * lax.rsqrt(jnp.mean(x * x, axis=-1, keepdims=True) + EPS) * g


def _sigmoid(x):
    return 1.0 / (1.0 + jnp.exp(-x))


def _silu(x):
    return x * _sigmoid(x)


def _mem_attention(qm, gm, kv_ref, out_ref):
    scale = MEM_HEAD_DIM ** -0.5 * LOG2E
    for hh in range(MEM_HEADS):
        lo, hi = hh * MEM_HEAD_DIM, (hh + 1) * MEM_HEAD_DIM
        s = _dot_nt(qm[:, lo:hi].astype(BF16), kv_ref[0, :, lo:hi]) * scale
        e = jnp.exp2(s - jnp.max(s, axis=-1, keepdims=True))
        o = _dot(e.astype(BF16), kv_ref[0, :, MEM_W + lo:MEM_W + hi])
        o = o / jnp.sum(e, axis=-1, keepdims=True)
        out_ref[0, :, lo:hi] = (o * _silu(gm[:, lo:hi])).astype(out_ref.dtype)


def _normed_with_halo(x_ref, xp_ref, xn_ref, g):
    h = _rms(x_ref[0], g).astype(BF16)
    hp = _rms(xp_ref[0], g).astype(BF16)
    hn = _rms(xn_ref[0], g).astype(BF16)
    return h, jnp.concatenate([hp, h, hn], axis=0)


def _halo_valid(tt):
    i = pl.program_id(1)
    last = pl.num_programs(1) - 1
    row = lax.broadcasted_iota(jnp.int32, (tt + 2 * HALO, 1), 0)
    return ((row >= HALO) | (i > 0)) & ((row < tt + HALO) | (i < last))


def _memkv_kernel(mem_ref, g_ref, w_ref, kv_ref):
    hn = _rms(mem_ref[...], g_ref[0]).astype(BF16)
    kv_ref[0] = _dot(hn, w_ref[0]).astype(BF16)


def _memkv(mem_all, mem_norm_g, w_mem_kv):
    depth = w_mem_kv.shape[0]
    nb = mem_all.shape[0] // N_MEM
    return pl.pallas_call(
        _memkv_kernel,
        grid=(depth, nb),
        in_specs=[
            pl.BlockSpec((N_MEM, D_MODEL), lambda l, b: (b, 0)),
            pl.BlockSpec((1, 1, D_MODEL), lambda l, b: (l, 0, 0)),
            pl.BlockSpec((1, D_MODEL, 2 * MEM_W), lambda l, b: (l, 0, 0)),
        ],
        out_specs=pl.BlockSpec((1, N_MEM, 2 * MEM_W), lambda l, b: (l, b, 0)),
        out_shape=jax.ShapeDtypeStruct((depth, nb * N_MEM, 2 * MEM_W), BF16),
        compiler_params=pltpu.CompilerParams(
            dimension_semantics=("arbitrary", "arbitrary"), vmem_limit_bytes=V7X_VMEM_LIMIT_BYTES),
        name="memkv",
    )(mem_all, mem_norm_g.reshape(depth, 1, D_MODEL), w_mem_kv)


def _even_front_kernel(x_ref, xp_ref, xn_ref, g_ref, w_ref, cw_ref, cb_ref, kv_ref,
                       ya_ref, q_ref, k_ref, v_ref, sgb_ref, ym_ref, p_scr):
    tt = x_ref.shape[1]
    h, h_ext = _normed_with_halo(x_ref, xp_ref, xn_ref, g_ref[...])

    cg = _dot(h_ext, w_ref[:, EV_CG:EV_CG + D_MODEL])
    xv = _dot(h_ext, w_ref[:, EV_XV:EV_XV + D_MODEL])
    p_scr[...] = jnp.where(_halo_valid(tt), cg * xv, 0.0)
    conv = cb_ref[...]
    for j in range(CONV_K):
        conv = conv + cw_ref[j:j + 1, :] * p_scr[pl.ds(HALO - CONV_K // 2 + j, tt), :]
    bg = _dot(h, w_ref[:, EV_BG:EV_BG + D_MODEL])
    ga = _dot(h, w_ref[:, EV_GA:EV_GA + D_MODEL])
    ya_ref[0] = (bg * conv * _silu(ga)).astype(BF16)

    q_ref[0] = (_dot(h, w_ref[:, EV_Q:EV_Q + D_MODEL]) * (NA_HEAD_DIM ** -0.5 * LOG2E)).astype(BF16)
    k_ref[0] = _dot(h, w_ref[:, EV_K:EV_K + D_MODEL]).astype(BF16)
    v_ref[0] = _dot(h, w_ref[:, EV_V:EV_V + D_MODEL]).astype(BF16)
    sgb_ref[0] = _silu(_dot(h, w_ref[:, EV_GB:EV_GB + D_MODEL])).astype(BF16)

    qm = _dot(h, w_ref[:, EV_QM:EV_QM + MEM_W])
    gm = _dot(h, w_ref[:, EV_GM:EV_GM + MEM_W])
    _mem_attention(qm, gm, kv_ref, ym_ref)


def _tile_specs(tt, t_len):
    per = tt // HALO
    n_halo = t_len // HALO
    return [
        pl.BlockSpec((1, tt, D_MODEL), lambda b, i: (b, i, 0)),
        pl.BlockSpec((1, HALO, D_MODEL), lambda b, i: (b, jnp.maximum(i * per - 1, 0), 0)),
        pl.BlockSpec((1, HALO, D_MODEL), lambda b, i: (b, jnp.minimum((i + 1) * per, n_halo - 1), 0)),
    ]


def _const_spec(shape):
    nd = len(shape)
    return pl.BlockSpec(shape, lambda b, i: (0,) * nd, pipeline_mode=pl.Buffered(1))


def _even_front(x, g, w_in, conv_w, conv_b, kv):
    bsz, t_len, _ = x.shape
    tt = min(TOKEN_TILE, t_len)
    act = lambda width: pl.BlockSpec((1, tt, width), lambda b, i: (b, i, 0))
    out_sd = lambda width: jax.ShapeDtypeStruct((bsz, t_len, width), BF16)
    return pl.pallas_call(
        _even_front_kernel,
        grid=(bsz, t_len // tt),
        in_specs=_tile_specs(tt, t_len) + [
            _const_spec((1, D_MODEL)),
            _const_spec(w_in.shape),
            _const_spec(conv_w.shape),
            _const_spec((1, D_MODEL)),
            pl.BlockSpec((1, N_MEM, 2 * MEM_W), lambda b, i: (b, 0, 0)),
        ],
        out_specs=[act(D_MODEL)] * 5 + [act(MEM_W)],
        out_shape=[out_sd(D_MODEL)] * 5 + [out_sd(MEM_W)],
        scratch_shapes=[pltpu.VMEM((tt + 2 * HALO, D_MODEL), F32)],
        compiler_params=pltpu.CompilerParams(
            dimension_semantics=("arbitrary", "arbitrary"), vmem_limit_bytes=V7X_VMEM_LIMIT_BYTES),
        name="even_front",
    )(x, x, x, g.reshape(1, D_MODEL), w_in, conv_w, conv_b.reshape(1, D_MODEL), kv)


def _na_bias_table(rpb):
    col = jnp.arange(GRID_W)
    cstart = jnp.clip(col - NA_WIN_COLS // 2, 0, GRID_W - NA_WIN_COLS)
    col_ok = (col[None, :] >= cstart[:, None]) & (col[None, :] < cstart[:, None] + NA_WIN_COLS)
    dc_idx = jnp.clip(col[None, :] - col[:, None], -(NA_WIN_COLS - 1), NA_WIN_COLS - 1) + NA_WIN_COLS - 1
    per_dr = jnp.where(col_ok[None, None], rpb.astype(F32)[:, :, dc_idx] * LOG2E, MASK_BIAS)
    per_dr = jnp.moveaxis(per_dr, 1, 0)
    two = jnp.concatenate([per_dr[:-1], per_dr[1:]], axis=-1)
    return two.reshape(2 * NA_WIN_ROWS - 2, NA_PAIRS, 2 * GRID_W, 2 * GRID_W)


def _even_back_kernel(q_ref, kp_ref, kc_ref, kn_ref, vp_ref, vc_ref, vn_ref, sgb_ref, ya_ref, ym_ref,
                      x_ref, u_ref, w_ref, out_ref, kwin, vwin, yb_scr, s_scr, m_scr):
    rbt = q_ref.shape[1]
    rb = rbt // GRID_W
    j = pl.program_id(1)
    rows = pl.num_programs(1) * rb
    win_keys = NA_WIN_ROWS * GRID_W

    kwin[0:rbt] = kp_ref[0]
    kwin[rbt:2 * rbt] = kc_ref[0]
    kwin[2 * rbt:3 * rbt] = kn_ref[0]
    vwin[0:rbt] = vp_ref[0]
    vwin[rbt:2 * rbt] = vc_ref[0]
    vwin[2 * rbt:3 * rbt] = vn_ref[0]

    lane = lax.broadcasted_iota(jnp.int32, (GRID_W, 2 * NA_HEAD_DIM), 1)
    first_head = lane < NA_HEAD_DIM

    independent = [(ya_ref, c, c) for c in range(0, D_MODEL, MXU_DEPTH)]
    independent += [(ym_ref, c, 2 * D_MODEL + c) for c in range(0, MEM_W, MXU_DEPTH)]
    out_ref[0] = x_ref[0]

    for i in range(rb):
        r = j * rb + i
        rstart = jnp.clip(r - NA_WIN_ROWS // 2, 0, rows - NA_WIN_ROWS)
        local = pl.multiple_of((rstart - (j - 1) * rb) * GRID_W, GRID_W)
        off = rstart - r + NA_WIN_ROWS - 1
        qrows = slice(i * GRID_W, (i + 1) * GRID_W)
        for p in range(NA_PAIRS):
            lo, hi = p * 2 * NA_HEAD_DIM, (p + 1) * 2 * NA_HEAD_DIM
            qp = q_ref[0, qrows, lo:hi]
            zero = jnp.zeros_like(qp)
            q2 = jnp.concatenate([jnp.where(first_head, qp, zero), jnp.where(first_head, zero, qp)], axis=0)
            s = _dot_nt(q2, kwin[pl.ds(local, win_keys), lo:hi])
            bias = jnp.concatenate([u_ref[off + 2 * m, p] for m in range(NA_WIN_ROWS // 2)], axis=1)
            s = s + bias
            s_scr[p] = s
            m_scr[p] = jnp.broadcast_to(jnp.max(s, axis=-1, keepdims=True), (2 * GRID_W, 2 * NA_HEAD_DIM))
        if independent:
            ref, c, wrow = independent.pop(0)
            out_ref[0] += _dot(ref[0, :, c:c + MXU_DEPTH], w_ref[wrow:wrow + MXU_DEPTH, :])
        for p in range(NA_PAIRS):
            lo, hi = p * 2 * NA_HEAD_DIM, (p + 1) * 2 * NA_HEAD_DIM
            m = m_scr[p]
            e = jnp.concatenate(
                [jnp.exp2(s_scr[p, :, c * 128:(c + 1) * 128] - m) for c in range(win_keys // 128)], axis=1)
            o2 = _dot(e.astype(BF16), vwin[pl.ds(local, win_keys), lo:hi])
            o2 = o2 / jnp.sum(e, axis=-1, keepdims=True)
            o = jnp.where(first_head, o2[:GRID_W], o2[GRID_W:])
            gate = sgb_ref[0, qrows, lo:hi].astype(F32)
            yb_scr[qrows, lo:hi] = (o * gate).astype(BF16)
    assert not independent

    out_ref[0] += _dot(yb_scr[...], w_ref[D_MODEL:2 * D_MODEL, :])


def _even_back(x, q, k, v, sgb, ya, ym, ubias, w_out):
    bsz, t_len, _ = x.shape
    rbt = NA_ROW_BLOCK * GRID_W
    nblk = t_len // rbt
    cur = lambda width: pl.BlockSpec((1, rbt, width), lambda b, j: (b, j, 0))
    prev = pl.BlockSpec((1, rbt, D_MODEL), lambda b, j: (b, jnp.maximum(j - 1, 0), 0))
    nxt = pl.BlockSpec((1, rbt, D_MODEL), lambda b, j: (b, jnp.minimum(j + 1, nblk - 1), 0))
    return pl.pallas_call(
        _even_back_kernel,
        grid=(bsz, nblk),
        in_specs=[cur(D_MODEL), prev, cur(D_MODEL), nxt, prev, cur(D_MODEL), nxt,
                  cur(D_MODEL), cur(D_MODEL), cur(MEM_W), cur(D_MODEL),
                  _const_spec(ubias.shape), _const_spec(w_out.shape)],
        out_specs=cur(D_MODEL),
        out_shape=jax.ShapeDtypeStruct((bsz, t_len, D_MODEL), F32),
        scratch_shapes=[pltpu.VMEM((3 * rbt, D_MODEL), BF16), pltpu.VMEM((3 * rbt, D_MODEL), BF16),
                        pltpu.VMEM((rbt, D_MODEL), BF16),
                        pltpu.VMEM((NA_PAIRS, 2 * GRID_W, NA_WIN_ROWS * GRID_W), F32),
                        pltpu.VMEM((NA_PAIRS, 2 * GRID_W, 2 * NA_HEAD_DIM), F32)],
        compiler_params=pltpu.CompilerParams(
            dimension_semantics=("arbitrary", "arbitrary"), vmem_limit_bytes=V7X_VMEM_LIMIT_BYTES),
        name="even_back",
    )(q, k, k, k, v, v, v, sgb, ya, ym, x, ubias, w_out)


def _lru_gates(xc, wg_ref, n):
    return _dot(xc.astype(BF16), wg_ref[n])


def _lru_coeffs(xc, pre, ba, bi, log2_a_scale):
    r = _sigmoid(pre[:, :LRU_BW] + ba)
    gate_i = _sigmoid(pre[:, LRU_BW:] + bi)
    a = jnp.exp2(r * log2_a_scale)
    return a, jnp.sqrt(1.0 - a * a) * (gate_i * xc)


def _softplus_neg(lam):
    return jnp.maximum(-lam, 0.0) + jnp.log(1.0 + jnp.exp(-jnp.abs(lam)))


def _lru_scan(a, b, carry, reverse):
    tt, width = a.shape
    ngroups = tt // SUBLANES
    a = a.reshape(ngroups, SUBLANES, width)
    b = b.reshape(ngroups, SUBLANES, width)
    pos = lax.broadcasted_iota(jnp.int32, (1, SUBLANES, width), 1)
    k = 1
    while k < SUBLANES:
        ok = (pos < SUBLANES - k) if reverse else (pos >= k)
        shift = SUBLANES - k if reverse else k
        a_s = jnp.where(ok, pltpu.roll(a, shift, 1), 1.0)
        b_s = jnp.where(ok, pltpu.roll(b, shift, 1), 0.0)
        b = a * b_s + b
        a = a * a_s
        k *= 2
    edge = 0 if reverse else SUBLANES - 1
    hs = [None] * ngroups
    for step in range(ngroups):
        g = (ngroups - 1 - step) if reverse else step
        hs[g] = a[g] * carry + b[g]
        carry = hs[g][edge:edge + 1, :]
    return jnp.concatenate(hs, axis=0), carry


def _odd_front_kernel(x_ref, xp_ref, xn_ref, g_ref, w_ref, cw_ref, cb_ref, wg_ref, ba_ref, bi_ref,
                      lam_ref, sg_ref, sw_ref, sbt_ref, kv_ref,
                      xc_ref, hf_ref, sgc_ref, yd_ref, ym_ref, xr_scr, carry_scr, z_scr):
    tt = x_ref.shape[1]
    h, h_ext = _normed_with_halo(x_ref, xp_ref, xn_ref, g_ref[...])

    @pl.when(pl.program_id(1) == 0)
    def _():
        carry_scr[...] = jnp.zeros_like(carry_scr)

    xr_scr[...] = jnp.where(_halo_valid(tt), _dot(h_ext, w_ref[:, OD_XR:OD_XR + D_MODEL]), 0.0)
    xc = cb_ref[...]
    for j in range(LRU_CONV_K):
        xc = xc + cw_ref[j:j + 1, :] * xr_scr[pl.ds(HALO - LRU_CONV_K // 2 + j, tt), :]
    xc_ref[0] = xc

    def project(col):
        z = _dot(h, w_ref[:, col:col + MXU_DEPTH])
        if col < OD_U:
            sgc_ref[0, :, col - OD_GC:col - OD_GC + MXU_DEPTH] = _silu(z).astype(BF16)
        else:
            z_scr[:, col - OD_U:col - OD_U + MXU_DEPTH] = z

    bulk = list(range(OD_GC, w_ref.shape[1], MXU_DEPTH))
    per_block = -(-len(bulk) // LRU_BLOCKS)
    log2_a_scale = (-LRU_C * LOG2E) * _softplus_neg(lam_ref[...])
    pre = _lru_gates(xc[:, 0:LRU_BW], wg_ref, 0)
    for n in range(LRU_BLOCKS):
        lo, hi = n * LRU_BW, (n + 1) * LRU_BW
        cur = pre
        if n + 1 < LRU_BLOCKS:
            pre = _lru_gates(xc[:, hi:hi + LRU_BW], wg_ref, n + 1)
        for col in bulk[n * per_block:(n + 1) * per_block]:
            project(col)
        a, b = _lru_coeffs(xc[:, lo:hi], cur, ba_ref[:, lo:hi], bi_ref[:, lo:hi], log2_a_scale[:, lo:hi])
        hf_ref[0, :, lo:hi], carry_scr[:, lo:hi] = _lru_scan(a, b, carry_scr[:, lo:hi], reverse=False)

    vn = _rms(z_scr[:, OD_V - OD_U:OD_GD - OD_U], sg_ref[...]).astype(BF16)
    for c in range(tt // SGU_CHUNK):
        r0, r1 = c * SGU_CHUNK, (c + 1) * SGU_CHUNK
        for grp in range(SGU_GROUPS):
            lo, hi = grp * LRU_BW, (grp + 1) * LRU_BW
            mixed = _dot(sw_ref[grp], vn[r0:r1, lo:hi]) + sbt_ref[:, grp:grp + 1]
            u = z_scr[r0:r1, lo:hi]
            sgd = _silu(z_scr[r0:r1, OD_GD - OD_U + lo:OD_GD - OD_U + hi])
            yd_ref[0, r0:r1, lo:hi] = (u * mixed * sgd).astype(BF16)

    _mem_attention(z_scr[:, OD_QM - OD_U:OD_GM - OD_U], z_scr[:, OD_GM - OD_U:], kv_ref, ym_ref)


def _odd_front(x, g, w_in, conv_w, conv_b, wg, ba, bi, lam, sgu_g, sgu_w, sgu_bt, kv):
    bsz, t_len, _ = x.shape
    tt = min(TOKEN_TILE, t_len)
    act = lambda width: pl.BlockSpec((1, tt, width), lambda b, i: (b, i, 0))
    sd = lambda width, dt: jax.ShapeDtypeStruct((bsz, t_len, width), dt)
    row = _const_spec((1, D_MODEL))
    return pl.pallas_call(
        _odd_front_kernel,
        grid=(bsz, t_len // tt),
        in_specs=_tile_specs(tt, t_len) + [
            row, _const_spec(w_in.shape), _const_spec(conv_w.shape), row,
            _const_spec(wg.shape), row, row, row,
            row, _const_spec(sgu_w.shape), _const_spec(sgu_bt.shape),
            pl.BlockSpec((1, N_MEM, 2 * MEM_W), lambda b, i: (b, 0, 0)),
        ],
        out_specs=[act(D_MODEL)] * 4 + [act(MEM_W)],
        out_shape=[sd(D_MODEL, F32), sd(D_MODEL, F32), sd(D_MODEL, BF16), sd(D_MODEL, BF16), sd(MEM_W, BF16)],
        scratch_shapes=[pltpu.VMEM((tt + 2 * HALO, D_MODEL), F32), pltpu.VMEM((1, D_MODEL), F32),
                        pltpu.VMEM((tt, w_in.shape[1] - OD_U), F32)],
        compiler_params=pltpu.CompilerParams(
            dimension_semantics=("arbitrary", "arbitrary"), vmem_limit_bytes=V7X_VMEM_LIMIT_BYTES),
        name="odd_front",
    )(x, x, x, g.reshape(1, D_MODEL), w_in, conv_w, conv_b.reshape(1, D_MODEL), wg,
      ba.reshape(1, D_MODEL), bi.reshape(1, D_MODEL), lam.reshape(1, D_MODEL),
      sgu_g.reshape(1, D_MODEL), sgu_w, sgu_bt, kv)


def _odd_back_kernel(xc_ref, hf_ref, sgc_ref, yd_ref, ym_ref, x_ref, wg_ref, ba_ref, bi_ref, lam_ref,
                     w_ref, fg_ref, out_ref, carry_scr):
    @pl.when(pl.program_id(1) == 0)
    def _():
        carry_scr[...] = jnp.zeros_like(carry_scr)

    independent = [(yd_ref, c, D_MODEL + c) for c in range(0, D_MODEL, MXU_DEPTH)]
    independent += [(ym_ref, c, 2 * D_MODEL + c) for c in range(0, MEM_W, MXU_DEPTH)]
    acc = x_ref[0]
    log2_a_scale = (-LRU_C * LOG2E) * _softplus_neg(lam_ref[...])
    yc = []
    pre = _lru_gates(xc_ref[0, :, 0:LRU_BW], wg_ref, 0)
    for n in range(LRU_BLOCKS):
        lo, hi = n * LRU_BW, (n + 1) * LRU_BW
        cur = pre
        if n + 1 < LRU_BLOCKS:
            pre = _lru_gates(xc_ref[0, :, hi:hi + LRU_BW], wg_ref, n + 1)
        if independent:
            ref, c, wrow = independent.pop(0)
            acc = acc + _dot(ref[0, :, c:c + MXU_DEPTH], w_ref[wrow:wrow + MXU_DEPTH, :])
        a, b = _lru_coeffs(xc_ref[0, :, lo:hi], cur, ba_ref[:, lo:hi], bi_ref[:, lo:hi], log2_a_scale[:, lo:hi])
        hb, carry_scr[:, lo:hi] = _lru_scan(a, b, carry_scr[:, lo:hi], reverse=True)
        yc.append(((hf_ref[0, :, lo:hi] + hb) * sgc_ref[0, :, lo:hi].astype(F32)).astype(BF16))
        if n % 2 == 1:
            acc = acc + _dot(jnp.concatenate(yc, axis=1), w_ref[lo - LRU_BW:hi, :])
            yc = []
    assert not independent
    out_ref[0] = _rms(acc, fg_ref[...])


def _odd_back(x, xc, hf, sgc, yd, ym, wg, ba, bi, lam, w_out, final_g):
    bsz, t_len, _ = x.shape
    tt = min(TOKEN_TILE, t_len)
    nt = t_len // tt
    act = lambda width: pl.BlockSpec((1, tt, width), lambda b, i: (b, nt - 1 - i, 0))
    row = _const_spec((1, D_MODEL))
    return pl.pallas_call(
        _odd_back_kernel,
        grid=(bsz, nt),
        in_specs=[act(D_MODEL)] * 4 + [act(MEM_W), act(D_MODEL),
                  _const_spec(wg.shape), row, row, row,
                  _const_spec(w_out.shape), row],
        out_specs=act(D_MODEL),
        out_shape=jax.ShapeDtypeStruct((bsz, t_len, D_MODEL), F32),
        scratch_shapes=[pltpu.VMEM((1, D_MODEL), F32)],
        compiler_params=pltpu.CompilerParams(
            dimension_semantics=("arbitrary", "arbitrary"), vmem_limit_bytes=V7X_VMEM_LIMIT_BYTES),
        name="odd_back",
    )(xc, hf, sgc, yd, ym, x, wg, ba.reshape(1, D_MODEL), bi.reshape(1, D_MODEL),
      lam.reshape(1, D_MODEL), w_out, final_g.reshape(1, D_MODEL))


def _trunk(x, kv, norm_g, ev_w_in, ev_conv_w, ev_conv_b, ubias, ev_w_out,
           od_w_in, od_conv_w, od_conv_b, od_lru_wa, od_lru_ba, od_lru_wi, od_lru_bi, od_lru_lam,
           od_sgu_g, od_sgu_w, od_sgu_bt, od_w_out, final_g):
    ya, q, k, v, sgb, ym = _even_front(x, norm_g[0], ev_w_in, ev_conv_w, ev_conv_b, kv[0])
    x = _even_back(x, q, k, v, sgb, ya, ym, ubias, ev_w_out)
    wg = jnp.concatenate([od_lru_wa, od_lru_wi], axis=-1)
    xc, hf, sgc, yd, ym = _odd_front(x, norm_g[1], od_w_in, od_conv_w, od_conv_b,
                                     wg[0], od_lru_ba[0], od_lru_bi[0], od_lru_lam[0],
                                     od_sgu_g, od_sgu_w, od_sgu_bt, kv[1])
    return _odd_back(x, xc, hf, sgc, yd, ym, wg[1], od_lru_ba[1], od_lru_bi[1],
                     od_lru_lam[1], od_w_out, final_g)


def kernel(x_prompt, x_sample, mem_prompt, mem_sample, norm_g, mem_norm_g, w_mem_kv, ev_w_in, ev_conv_w,
           ev_conv_b, ev_rpb, ev_w_out, od_w_in, od_conv_w, od_conv_b, od_lru_wa, od_lru_ba, od_lru_wi,
           od_lru_bi, od_lru_lam, od_sgu_g, od_sgu_w, od_sgu_b, od_w_out, final_g):
    depth = norm_g.shape[0]
    assert depth == 2 and ev_w_in.shape[0] == 1 and od_w_in.shape[0] == 1
    nb_p, nb_s = mem_prompt.shape[0], mem_sample.shape[0]
    mem_all = jnp.concatenate([mem_prompt.reshape(-1, D_MODEL), mem_sample.reshape(-1, D_MODEL)], axis=0)
    kv = _memkv(mem_all, mem_norm_g, w_mem_kv.astype(BF16)).reshape(depth, nb_p + nb_s, N_MEM, 2 * MEM_W)
    shared = (norm_g, ev_w_in[0].astype(BF16), ev_conv_w[0], ev_conv_b[0], _na_bias_table(ev_rpb[0]),
              ev_w_out[0].astype(BF16), od_w_in[0].astype(BF16), od_conv_w[0], od_conv_b[0],
              od_lru_wa[0].astype(BF16), od_lru_ba[0], od_lru_wi[0].astype(BF16), od_lru_bi[0], od_lru_lam[0],
              od_sgu_g[0], od_sgu_w[0].astype(BF16), od_sgu_b[0].T, od_w_out[0].astype(BF16), final_g)
    y_prompt = _trunk(x_prompt, kv[:, :nb_p], *shared)
    y_sample = _trunk(x_sample, kv[:, nb_p:], *shared)
    return (y_prompt, y_sample)
```

```python
import functools

import jax
import jax.numpy as jnp
from jax import lax
from jax.experimental import pallas as pl
from jax.experimental.pallas import tpu as pltpu

F32 = jnp.float32
BF16 = jnp.bfloat16

D_MODEL = 1024
GRID_W = 64
N_MEM = 256
EPS = 1e-6
CONV_K = 3
NA_HEADS = 16
NA_HEAD_DIM = 64
NA_WIN_ROWS = 8
NA_WIN_COLS = 16
LRU_BLOCKS = 8
LRU_BW = 128
LRU_CONV_K = 4
LRU_C = 8.0
SGU_GROUPS = 8
SGU_CHUNK = 128
MEM_HEADS = 4
MEM_HEAD_DIM = 128
MEM_W = MEM_HEADS * MEM_HEAD_DIM

EV_BG, EV_CG, EV_XV, EV_GA, EV_Q, EV_K, EV_V, EV_GB, EV_QM, EV_GM = (
    0, 1024, 2048, 3072, 4096, 5120, 6144, 7168, 8192, 8704)
OD_XR, OD_GC, OD_U, OD_V, OD_GD, OD_QM, OD_GM = (0, 1024, 2048, 3072, 4096, 5120, 5632)

V7X_VMEM_LIMIT_BYTES = 60 * 1024 * 1024
SUBLANES = 8
MXU_DEPTH = 256
HALO = 16
TOKEN_TILE = 512
NA_ROW_BLOCK = 8
NA_PAIRS = NA_HEADS // 2
MASK_BIAS = -1e30
LOG2E = 1.4426950408889634


def _dot(a, b):
    return jnp.dot(a, b, preferred_element_type=F32)


def _dot_nt(a, b):
    return lax.dot_general(a, b, (((1,), (1,)), ((), ())), preferred_element_type=F32)


def _rms(x, g):
    return x * lax.rsqrt(jnp.mean(x * x, axis=-1, keepdims=True) + EPS) * g


def _sigmoid(x):
    return 1.0 / (1.0 + jnp.exp(-x))


def _silu(x):
    return x * _sigmoid(x)


def _mem_attention(qm, gm, kv_ref, out_ref):
    scale = MEM_HEAD_DIM ** -0.5 * LOG2E
    for hh in range(MEM_HEADS):
        lo, hi = hh * MEM_HEAD_DIM, (hh + 1) * MEM_HEAD_DIM
        s = _dot_nt(qm[:, lo:hi].astype(BF16), kv_ref[0, :, lo:hi]) * scale
        e = jnp.exp2(s - jnp.max(s, axis=-1, keepdims=True))
        o = _dot(e.astype(BF16), kv_ref[0, :, MEM_W + lo:MEM_W + hi])
        o = o / jnp.sum(e, axis=-1, keepdims=True)
        out_ref[0, :, lo:hi] = (o * _silu(gm[:, lo:hi])).astype(out_ref.dtype)


def _normed_with_halo(x_ref, xp_ref, xn_ref, g):
    h = _rms(x_ref[0], g).astype(BF16)
    hp = _rms(xp_ref[0], g).astype(BF16)
    hn = _rms(xn_ref[0], g).astype(BF16)
    return h, jnp.concatenate([hp, h, hn], axis=0)


def _halo_valid(tt):
    i = pl.program_id(1)
    last = pl.num_programs(1) - 1
    row = lax.broadcasted_iota(jnp.int32, (tt + 2 * HALO, 1), 0)
    return ((row >= HALO) | (i > 0)) & ((row < tt + HALO) | (i < last))


def _memkv_kernel(mem_ref, g_ref, w_ref, kv_ref):
    hn = _rms(mem_ref[...], g_ref[0]).astype(BF16)
    kv_ref[0] = _dot(hn, w_ref[0]).astype(BF16)


def _memkv(mem_all, mem_norm_g, w_mem_kv):
    depth = w_mem_kv.shape[0]
    nb = mem_all.shape[0] // N_MEM
    return pl.pallas_call(
        _memkv_kernel,
        grid=(depth, nb),
        in_specs=[
            pl.BlockSpec((N_MEM, D_MODEL), lambda l, b: (b, 0)),
            pl.BlockSpec((1, 1, D_MODEL), lambda l, b: (l, 0, 0)),
            pl.BlockSpec((1, D_MODEL, 2 * MEM_W), lambda l, b: (l, 0, 0)),
        ],
        out_specs=pl.BlockSpec((1, N_MEM, 2 * MEM_W), lambda l, b: (l, b, 0)),
        out_shape=jax.ShapeDtypeStruct((depth, nb * N_MEM, 2 * MEM_W), BF16),
        compiler_params=pltpu.CompilerParams(
            dimension_semantics=("arbitrary", "arbitrary"), vmem_limit_bytes=V7X_VMEM_LIMIT_BYTES),
        name="memkv",
    )(mem_all, mem_norm_g.reshape(depth, 1, D_MODEL), w_mem_kv)


def _even_front_kernel(x_ref, xp_ref, xn_ref, g_ref, w_ref, cw_ref, cb_ref, kv_ref,
                       ya_ref, q_ref, k_ref, v_ref, sgb_ref, ym_ref, p_scr):
    tt = x_ref.shape[1]
    h, h_ext = _normed_with_halo(x_ref, xp_ref, xn_ref, g_ref[...])

    cg = _dot(h_ext, w_ref[:, EV_CG:EV_CG + D_MODEL])
    xv = _dot(h_ext, w_ref[:, EV_XV:EV_XV + D_MODEL])
    p_scr[...] = jnp.where(_halo_valid(tt), cg * xv, 0.0)
    conv = cb_ref[...]
    for j in range(CONV_K):
        conv = conv + cw_ref[j:j + 1, :] * p_scr[pl.ds(HALO - CONV_K // 2 + j, tt), :]
    bg = _dot(h, w_ref[:, EV_BG:EV_BG + D_MODEL])
    ga = _dot(h, w_ref[:, EV_GA:EV_GA + D_MODEL])
    ya_ref[0] = (bg * conv * _silu(ga)).astype(BF16)

    q_ref[0] = (_dot(h, w_ref[:, EV_Q:EV_Q + D_MODEL]) * (NA_HEAD_DIM ** -0.5 * LOG2E)).astype(BF16)
    k_ref[0] = _dot(h, w_ref[:, EV_K:EV_K + D_MODEL]).astype(BF16)
    v_ref[0] = _dot(h, w_ref[:, EV_V:EV_V + D_MODEL]).astype(BF16)
    sgb_ref[0] = _silu(_dot(h, w_ref[:, EV_GB:EV_GB + D_MODEL])).astype(BF16)

    qm = _dot(h, w_ref[:, EV_QM:EV_QM + MEM_W])
    gm = _dot(h, w_ref[:, EV_GM:EV_GM + MEM_W])
    _mem_attention(qm, gm, kv_ref, ym_ref)


def _tile_specs(tt, t_len):
    per = tt // HALO
    n_halo = t_len // HALO
    return [
        pl.BlockSpec((1, tt, D_MODEL), lambda b, i: (b, i, 0)),
        pl.BlockSpec((1, HALO, D_MODEL), lambda b, i: (b, jnp.maximum(i * per - 1, 0), 0)),
        pl.BlockSpec((1, HALO, D_MODEL), lambda b, i: (b, jnp.minimum((i + 1) * per, n_halo - 1), 0)),
    ]


def _const_spec(shape):
    nd = len(shape)
    return pl.BlockSpec(shape, lambda b, i: (0,) * nd, pipeline_mode=pl.Buffered(1))


def _even_front(x, g, w_in, conv_w, conv_b, kv):
    bsz, t_len, _ = x.shape
    tt = min(TOKEN_TILE, t_len)
    act = lambda width: pl.BlockSpec((1, tt, width), lambda b, i: (b, i, 0))
    out_sd = lambda width: jax.ShapeDtypeStruct((bsz, t_len, width), BF16)
    return pl.pallas_call(
        _even_front_kernel,
        grid=(bsz, t_len // tt),
        in_specs=_tile_specs(tt, t_len) + [
            _const_spec((1, D_MODEL)),
            _const_spec(w_in.shape),
            _const_spec(conv_w.shape),
            _const_spec((1, D_MODEL)),
            pl.BlockSpec((1, N_MEM, 2 * MEM_W), lambda b, i: (b, 0, 0)),
        ],
        out_specs=[act(D_MODEL)] * 5 + [act(MEM_W)],
        out_shape=[out_sd(D_MODEL)] * 5 + [out_sd(MEM_W)],
        scratch_shapes=[pltpu.VMEM((tt + 2 * HALO, D_MODEL), F32)],
        compiler_params=pltpu.CompilerParams(
            dimension_semantics=("arbitrary", "arbitrary"), vmem_limit_bytes=V7X_VMEM_LIMIT_BYTES),
        name="even_front",
    )(x, x, x, g.reshape(1, D_MODEL), w_in, conv_w, conv_b.reshape(1, D_MODEL), kv)


def _na_bias_table(rpb):
    col = jnp.arange(GRID_W)
    cstart = jnp.clip(col - NA_WIN_COLS // 2, 0, GRID_W - NA_WIN_COLS)
    col_ok = (col[None, :] >= cstart[:, None]) & (col[None, :] < cstart[:, None] + NA_WIN_COLS)
    dc_idx = jnp.clip(col[None, :] - col[:, None], -(NA_WIN_COLS - 1), NA_WIN_COLS - 1) + NA_WIN_COLS - 1
    per_dr = jnp.where(col_ok[None, None], rpb.astype(F32)[:, :, dc_idx] * LOG2E, MASK_BIAS)
    per_dr = jnp.moveaxis(per_dr, 1, 0)
    two = jnp.concatenate([per_dr[:-1], per_dr[1:]], axis=-1)
    return two.reshape(2 * NA_WIN_ROWS - 2, NA_PAIRS, 2 * GRID_W, 2 * GRID_W)


def _even_back_kernel(q_ref, k0_ref, kn_ref, v0_ref, vn_ref, sgb_ref, ya_ref, ym_ref,
                      x_ref, u_ref, w_ref, out_ref, kwin, vwin, yb_scr, s_scr, m_scr):
    rbt = q_ref.shape[1]
    rb = rbt // GRID_W
    j = pl.program_id(1)
    rows = pl.num_programs(1) * rb
    win_keys = NA_WIN_ROWS * GRID_W

    @pl.when(j == 0)
    def _():
        kwin[rbt:2 * rbt] = k0_ref[0]
        vwin[rbt:2 * rbt] = v0_ref[0]

    @pl.when(j > 0)
    def _():
        kwin[0:2 * rbt] = kwin[rbt:3 * rbt]
        vwin[0:2 * rbt] = vwin[rbt:3 * rbt]

    kwin[2 * rbt:3 * rbt] = kn_ref[0]
    vwin[2 * rbt:3 * rbt] = vn_ref[0]

    lane = lax.broadcasted_iota(jnp.int32, (GRID_W, 2 * NA_HEAD_DIM), 1)
    first_head = lane < NA_HEAD_DIM

    independent = [(ya_ref, c, c) for c in range(0, D_MODEL, MXU_DEPTH)]
    independent += [(ym_ref, c, 2 * D_MODEL + c) for c in range(0, MEM_W, MXU_DEPTH)]
    out_ref[0] = x_ref[0]

    for i in range(rb):
        r = j * rb + i
        rstart = jnp.clip(r - NA_WIN_ROWS // 2, 0, rows - NA_WIN_ROWS)
        local = pl.multiple_of((rstart - (j - 1) * rb) * GRID_W, GRID_W)
        off = rstart - r + NA_WIN_ROWS - 1
        qrows = slice(i * GRID_W, (i + 1) * GRID_W)
        for p in range(NA_PAIRS):
            lo, hi = p * 2 * NA_HEAD_DIM, (p + 1) * 2 * NA_HEAD_DIM
            qp = q_ref[0, qrows, lo:hi]
            zero = jnp.zeros_like(qp)
            q2 = jnp.concatenate([jnp.where(first_head, qp, zero), jnp.where(first_head, zero, qp)], axis=0)
            s = _dot_nt(q2, kwin[pl.ds(local, win_keys), lo:hi])
            bias = jnp.concatenate([u_ref[off + 2 * m, p] for m in range(NA_WIN_ROWS // 2)], axis=1)
            s = s + bias
            s_scr[p] = s
            m_scr[p] = jnp.broadcast_to(jnp.max(s, axis=-1, keepdims=True), (2 * GRID_W, 2 * NA_HEAD_DIM))
        if independent:
            ref, c, wrow = independent.pop(0)
            out_ref[0] += _dot(ref[0, :, c:c + MXU_DEPTH], w_ref[wrow:wrow + MXU_DEPTH, :])
        for p in range(NA_PAIRS):
            lo, hi = p * 2 * NA_HEAD_DIM, (p + 1) * 2 * NA_HEAD_DIM
            m = m_scr[p]
            e = jnp.concatenate(
                [jnp.exp2(s_scr[p, :, c * 128:(c + 1) * 128] - m) for c in range(win_keys // 128)], axis=1)
            o2 = _dot(e.astype(BF16), vwin[pl.ds(local, win_keys), lo:hi])
            o2 = o2 / jnp.sum(e, axis=-1, keepdims=True)
            o = jnp.where(first_head, o2[:GRID_W], o2[GRID_W:])
            gate = sgb_ref[0, qrows, lo:hi].astype(F32)
            yb_scr[qrows, lo:hi] = (o * gate).astype(BF16)
    assert not independent

    out_ref[0] += _dot(yb_scr[...], w_ref[D_MODEL:2 * D_MODEL, :])


def _even_back(x, q, k, v, sgb, ya, ym, ubias, w_out):
    bsz, t_len, _ = x.shape
    rbt = NA_ROW_BLOCK * GRID_W
    nblk = t_len // rbt
    cur = lambda width: pl.BlockSpec((1, rbt, width), lambda b, j: (b, j, 0))
    first = pl.BlockSpec((1, rbt, D_MODEL), lambda b, j: (b, 0, 0))
    nxt = pl.BlockSpec((1, rbt, D_MODEL), lambda b, j: (b, jnp.minimum(j + 1, nblk - 1), 0))
    return pl.pallas_call(
        _even_back_kernel,
        grid=(bsz, nblk),
        in_specs=[cur(D_MODEL), first, nxt, first, nxt,
                  cur(D_MODEL), cur(D_MODEL), cur(MEM_W), cur(D_MODEL),
                  _const_spec(ubias.shape), _const_spec(w_out.shape)],
        out_specs=cur(D_MODEL),
        out_shape=jax.ShapeDtypeStruct((bsz, t_len, D_MODEL), F32),
        scratch_shapes=[pltpu.VMEM((3 * rbt, D_MODEL), BF16), pltpu.VMEM((3 * rbt, D_MODEL), BF16),
                        pltpu.VMEM((rbt, D_MODEL), BF16),
                        pltpu.VMEM((NA_PAIRS, 2 * GRID_W, NA_WIN_ROWS * GRID_W), F32),
                        pltpu.VMEM((NA_PAIRS, 2 * GRID_W, 2 * NA_HEAD_DIM), F32)],
        compiler_params=pltpu.CompilerParams(
            dimension_semantics=("arbitrary", "arbitrary"), vmem_limit_bytes=V7X_VMEM_LIMIT_BYTES),
        name="even_back",
    )(q, k, k, v, v, sgb, ya, ym, x, ubias, w_out)


def _lru_gates(xc, wg_ref, n):
    return _dot(xc.astype(BF16), wg_ref[n])


def _lru_coeffs(xc, pre, ba, bi, log2_a_scale):
    r = _sigmoid(pre[:, :LRU_BW] + ba)
    gate_i = _sigmoid(pre[:, LRU_BW:] + bi)
    a = jnp.exp2(r * log2_a_scale)
    return a, jnp.sqrt(1.0 - a * a) * (gate_i * xc)


def _softplus_neg(lam):
    return jnp.maximum(-lam, 0.0) + jnp.log(1.0 + jnp.exp(-jnp.abs(lam)))


def _segment_order(tt):
    p = jnp.arange(tt)
    t = (p % SUBLANES) * (tt // SUBLANES) + p // SUBLANES
    return (t[:, None] == jnp.arange(tt)[None, :]).astype(BF16)


def _scan_sublanes(a, b, reverse):
    pos = lax.broadcasted_iota(jnp.int32, a.shape, 0)
    k = 1
    while k < SUBLANES:
        ok = (pos < SUBLANES - k) if reverse else (pos >= k)
        shift = SUBLANES - k if reverse else k
        a_s = jnp.where(ok, pltpu.roll(a, shift, 0), 1.0)
        b_s = jnp.where(ok, pltpu.roll(b, shift, 0), 0.0)
        b = a * b_s + b
        a = a * a_s
        k *= 2
    return a, b


def _lru_scan(a, b, carry, reverse):
    tt, width = a.shape
    seg = tt // SUBLANES
    a = a.reshape(seg, SUBLANES, width)
    b = b.reshape(seg, SUBLANES, width)
    steps = range(seg - 1, -1, -1) if reverse else range(seg)
    a_tot, h_tot = None, None
    for j in steps:
        if a_tot is None:
            a_tot, h_tot = a[j], b[j]
        else:
            a_tot, h_tot = a_tot * a[j], a[j] * h_tot + b[j]
    a_cum, h_cum = _scan_sublanes(a_tot, h_tot, reverse)
    leaving = a_cum * carry + h_cum
    pos = lax.broadcasted_iota(jnp.int32, leaving.shape, 0)
    if reverse:
        h = jnp.where(pos == SUBLANES - 1, carry, pltpu.roll(leaving, SUBLANES - 1, 0))
        carry = leaving[0:1, :]
    else:
        h = jnp.where(pos == 0, carry, pltpu.roll(leaving, 1, 0))
        carry = leaving[SUBLANES - 1:SUBLANES, :]
    hs = [None] * seg
    for j in steps:
        h = a[j] * h + b[j]
        hs[j] = h
    return jnp.concatenate(hs, axis=0), carry


def _odd_front_kernel(x_ref, xp_ref, xn_ref, g_ref, w_ref, cw_ref, cb_ref, wg_ref, ba_ref, bi_ref,
                      lam_ref, sg_ref, sw_ref, sbt_ref, kv_ref, perm_ref,
                      xc_ref, hf_ref, sgc_ref, yd_ref, ym_ref, xr_scr, carry_scr, z_scr):
    tt = x_ref.shape[1]
    seg = tt // SUBLANES
    i = pl.program_id(1)
    h, h_ext = _normed_with_halo(x_ref, xp_ref, xn_ref, g_ref[...])
    hs = _dot(perm_ref[...], h).astype(BF16)

    @pl.when(i == 0)
    def _():
        carry_scr[...] = jnp.zeros_like(carry_scr)

    lhs = jnp.concatenate([hs, h_ext[:HALO], h_ext[HALO + tt:]], axis=0)
    xr = _dot(lhs, w_ref[:, OD_XR:OD_XR + D_MODEL])
    before = jnp.where(i > 0, xr[tt + HALO - 2:tt + HALO], 0.0)
    after = jnp.where(i < pl.num_programs(1) - 1, xr[tt + HALO:tt + HALO + 1], 0.0)
    sub = lax.broadcasted_iota(jnp.int32, (SUBLANES, 1), 0)
    lead = LRU_CONV_K // 2
    for d in range(lead):
        slab = pltpu.roll(xr[(seg + d - lead) * SUBLANES:(seg + d - lead + 1) * SUBLANES], 1, 0)
        xr_scr[d * SUBLANES:(d + 1) * SUBLANES] = jnp.where(sub == 0, before[d:d + 1], slab)
    xr_scr[lead * SUBLANES:lead * SUBLANES + tt] = xr[:tt]
    slab = pltpu.roll(xr[0:SUBLANES], SUBLANES - 1, 0)
    xr_scr[lead * SUBLANES + tt:] = jnp.where(sub == SUBLANES - 1, after, slab)
    xc = cb_ref[...]
    for j in range(LRU_CONV_K):
        xc = xc + cw_ref[j:j + 1, :] * xr_scr[j * SUBLANES:j * SUBLANES + tt, :]
    xc_ref[0] = xc

    def project(col):
        if col < OD_U:
            z = _dot(hs, w_ref[:, col:col + MXU_DEPTH])
            sgc_ref[0, :, col - OD_GC:col - OD_GC + MXU_DEPTH] = _silu(z).astype(BF16)
        else:
            z_scr[:, col - OD_U:col - OD_U + MXU_DEPTH] = _dot(h, w_ref[:, col:col + MXU_DEPTH])

    bulk = list(range(OD_GC, w_ref.shape[1], MXU_DEPTH))
    per_block = -(-len(bulk) // LRU_BLOCKS)
    log2_a_scale = (-LRU_C * LOG2E) * _softplus_neg(lam_ref[...])
    pre = _lru_gates(xc[:, 0:LRU_BW], wg_ref, 0)
    for n in range(LRU_BLOCKS):
        lo, hi = n * LRU_BW, (n + 1) * LRU_BW
        cur = pre
        if n + 1 < LRU_BLOCKS:
            pre = _lru_gates(xc[:, hi:hi + LRU_BW], wg_ref, n + 1)
        for col in bulk[n * per_block:(n + 1) * per_block]:
            project(col)
        a, b = _lru_coeffs(xc[:, lo:hi], cur, ba_ref[:, lo:hi], bi_ref[:, lo:hi], log2_a_scale[:, lo:hi])
        hf_ref[0, :, lo:hi], carry_scr[:, lo:hi] = _lru_scan(a, b, carry_scr[:, lo:hi], reverse=False)

    vn = _rms(z_scr[:, OD_V - OD_U:OD_GD - OD_U], sg_ref[...]).astype(BF16)
    for c in range(tt // SGU_CHUNK):
        r0, r1 = c * SGU_CHUNK, (c + 1) * SGU_CHUNK
        for grp in range(SGU_GROUPS):
            lo, hi = grp * LRU_BW, (grp + 1) * LRU_BW
            mixed = _dot(sw_ref[grp], vn[r0:r1, lo:hi]) + sbt_ref[:, grp:grp + 1]
            u = z_scr[r0:r1, lo:hi]
            sgd = _silu(z_scr[r0:r1, OD_GD - OD_U + lo:OD_GD - OD_U + hi])
            yd_ref[0, r0:r1, lo:hi] = (u * mixed * sgd).astype(BF16)

    _mem_attention(z_scr[:, OD_QM - OD_U:OD_GM - OD_U], z_scr[:, OD_GM - OD_U:], kv_ref, ym_ref)


def _odd_front(x, g, w_in, conv_w, conv_b, wg, ba, bi, lam, sgu_g, sgu_w, sgu_bt, kv):
    bsz, t_len, _ = x.shape
    tt = min(TOKEN_TILE, t_len)
    act = lambda width: pl.BlockSpec((1, tt, width), lambda b, i: (b, i, 0))
    sd = lambda width, dt: jax.ShapeDtypeStruct((bsz, t_len, width), dt)
    row = _const_spec((1, D_MODEL))
    return pl.pallas_call(
        _odd_front_kernel,
        grid=(bsz, t_len // tt),
        in_specs=_tile_specs(tt, t_len) + [
            row, _const_spec(w_in.shape), _const_spec(conv_w.shape), row,
            _const_spec(wg.shape), row, row, row,
            row, _const_spec(sgu_w.shape), _const_spec(sgu_bt.shape),
            pl.BlockSpec((1, N_MEM, 2 * MEM_W), lambda b, i: (b, 0, 0)),
            _const_spec((tt, tt)),
        ],
        out_specs=[act(D_MODEL)] * 4 + [act(MEM_W)],
        out_shape=[sd(D_MODEL, F32), sd(D_MODEL, F32), sd(D_MODEL, BF16), sd(D_MODEL, BF16), sd(MEM_W, BF16)],
        scratch_shapes=[pltpu.VMEM((tt + (LRU_CONV_K - 1) * SUBLANES, D_MODEL), F32),
                        pltpu.VMEM((1, D_MODEL), F32),
                        pltpu.VMEM((tt, w_in.shape[1] - OD_U), F32)],
        compiler_params=pltpu.CompilerParams(
            dimension_semantics=("arbitrary", "arbitrary"), vmem_limit_bytes=V7X_VMEM_LIMIT_BYTES),
        name="odd_front",
    )(x, x, x, g.reshape(1, D_MODEL), w_in, conv_w, conv_b.reshape(1, D_MODEL), wg,
      ba.reshape(1, D_MODEL), bi.reshape(1, D_MODEL), lam.reshape(1, D_MODEL),
      sgu_g.reshape(1, D_MODEL), sgu_w, sgu_bt, kv, _segment_order(tt))


def _odd_back_kernel(xc_ref, hf_ref, sgc_ref, yd_ref, ym_ref, x_ref, wg_ref, ba_ref, bi_ref, lam_ref,
                     w_ref, fg_ref, unperm_ref, out_ref, carry_scr):
    @pl.when(pl.program_id(1) == 0)
    def _():
        carry_scr[...] = jnp.zeros_like(carry_scr)

    independent = [(yd_ref, c, D_MODEL + c) for c in range(0, D_MODEL, MXU_DEPTH)]
    independent += [(ym_ref, c, 2 * D_MODEL + c) for c in range(0, MEM_W, MXU_DEPTH)]
    acc = x_ref[0]
    log2_a_scale = (-LRU_C * LOG2E) * _softplus_neg(lam_ref[...])
    yc = []
    pre = _lru_gates(xc_ref[0, :, 0:LRU_BW], wg_ref, 0)
    for n in range(LRU_BLOCKS):
        lo, hi = n * LRU_BW, (n + 1) * LRU_BW
        cur = pre
        if n + 1 < LRU_BLOCKS:
            pre = _lru_gates(xc_ref[0, :, hi:hi + LRU_BW], wg_ref, n + 1)
        if independent:
            ref, c, wrow = independent.pop(0)
            acc = acc + _dot(ref[0, :, c:c + MXU_DEPTH], w_ref[wrow:wrow + MXU_DEPTH, :])
        a, b = _lru_coeffs(xc_ref[0, :, lo:hi], cur, ba_ref[:, lo:hi], bi_ref[:, lo:hi], log2_a_scale[:, lo:hi])
        hb, carry_scr[:, lo:hi] = _lru_scan(a, b, carry_scr[:, lo:hi], reverse=True)
        yc.append(((hf_ref[0, :, lo:hi] + hb) * sgc_ref[0, :, lo:hi].astype(F32)).astype(BF16))
        if n % 2 == 1:
            yc_time = _dot(unperm_ref[...], jnp.concatenate(yc, axis=1)).astype(BF16)
            acc = acc + _dot(yc_time, w_ref[lo - LRU_BW:hi, :])
            yc = []
    assert not independent
    out_ref[0] = _rms(acc, fg_ref[...])


def _odd_back(x, xc, hf, sgc, yd, ym, wg, ba, bi, lam, w_out, final_g):
    bsz, t_len, _ = x.shape
    tt = min(TOKEN_TILE, t_len)
    nt = t_len // tt
    act = lambda width: pl.BlockSpec((1, tt, width), lambda b, i: (b, nt - 1 - i, 0))
    row = _const_spec((1, D_MODEL))
    return pl.pallas_call(
        _odd_back_kernel,
        grid=(bsz, nt),
        in_specs=[act(D_MODEL)] * 4 + [act(MEM_W), act(D_MODEL),
                  _const_spec(wg.shape), row, row, row,
                  _const_spec(w_out.shape), row, _const_spec((tt, tt))],
        out_specs=act(D_MODEL),
        out_shape=jax.ShapeDtypeStruct((bsz, t_len, D_MODEL), F32),
        scratch_shapes=[pltpu.VMEM((1, D_MODEL), F32)],
        compiler_params=pltpu.CompilerParams(
            dimension_semantics=("arbitrary", "arbitrary"), vmem_limit_bytes=V7X_VMEM_LIMIT_BYTES),
        name="odd_back",
    )(xc, hf, sgc, yd, ym, x, wg, ba.reshape(1, D_MODEL), bi.reshape(1, D_MODEL),
      lam.reshape(1, D_MODEL), w_out, final_g.reshape(1, D_MODEL), _segment_order(tt).T)


def _trunk(x, kv, norm_g, ev_w_in, ev_conv_w, ev_conv_b, ubias, ev_w_out,
           od_w_in, od_conv_w, od_conv_b, od_lru_wa, od_lru_ba, od_lru_wi, od_lru_bi, od_lru_lam,
           od_sgu_g, od_sgu_w, od_sgu_bt, od_w_out, final_g):
    ya, q, k, v, sgb, ym = _even_front(x, norm_g[0], ev_w_in, ev_conv_w, ev_conv_b, kv[0])
    x = _even_back(x, q, k, v, sgb, ya, ym, ubias, ev_w_out)
    wg = jnp.concatenate([od_lru_wa, od_lru_wi], axis=-1)
    xc, hf, sgc, yd, ym = _odd_front(x, norm_g[1], od_w_in, od_conv_w, od_conv_b,
                                     wg[0], od_lru_ba[0], od_lru_bi[0], od_lru_lam[0],
                                     od_sgu_g, od_sgu_w, od_sgu_bt, kv[1])
    return _odd_back(x, xc, hf, sgc, yd, ym, wg[1], od_lru_ba[1], od_lru_bi[1],
                     od_lru_lam[1], od_w_out, final_g)


def kernel(x_prompt, x_sample, mem_prompt, mem_sample, norm_g, mem_norm_g, w_mem_kv, ev_w_in, ev_conv_w,
           ev_conv_b, ev_rpb, ev_w_out, od_w_in, od_conv_w, od_conv_b, od_lru_wa, od_lru_ba, od_lru_wi,
           od_lru_bi, od_lru_lam, od_sgu_g, od_sgu_w, od_sgu_b, od_w_out, final_g):
    depth = norm_g.shape[0]
    assert depth == 2 and ev_w_in.shape[0] == 1 and od_w_in.shape[0] == 1
    nb_p, nb_s = mem_prompt.shape[0], mem_sample.shape[0]
    mem_all = jnp.concatenate([mem_prompt.reshape(-1, D_MODEL), mem_sample.reshape(-1, D_MODEL)], axis=0)
    kv = _memkv(mem_all, mem_norm_g, w_mem_kv.astype(BF16)).reshape(depth, nb_p + nb_s, N_MEM, 2 * MEM_W)
    shared = (norm_g, ev_w_in[0].astype(BF16), ev_conv_w[0], ev_conv_b[0], _na_bias_table(ev_rpb[0]),
              ev_w_out[0].astype(BF16), od_w_in[0].astype(BF16), od_conv_w[0], od_conv_b[0],
              od_lru_wa[0].astype(BF16), od_lru_ba[0], od_lru_wi[0].astype(BF16), od_lru_bi[0], od_lru_lam[0],
              od_sgu_g[0], od_sgu_w[0].astype(BF16), od_sgu_b[0].T, od_w_out[0].astype(BF16), final_g)
    y_prompt = _trunk(x_prompt, kv[:, :nb_p], *shared)
    y_sample = _trunk(x_sample, kv[:, nb_p:], *shared)
    return (y_prompt, y_sample)
```

```python
import functools

import jax
import jax.numpy as jnp
import numpy as np
from jax import lax
from jax.experimental import pallas as pl
from jax.experimental.pallas import tpu as pltpu

F32 = jnp.float32
BF16 = jnp.bfloat16

D_MODEL = 1024
GRID_W = 64
N_MEM = 256
EPS = 1e-6
CONV_K = 3
NA_HEADS = 16
NA_HEAD_DIM = 64
NA_WIN_ROWS = 8
NA_WIN_COLS = 16
LRU_BLOCKS = 8
LRU_BW = 128
LRU_CONV_K = 4
LRU_C = 8.0
SGU_GROUPS = 8
SGU_CHUNK = 128
MEM_HEADS = 4
MEM_HEAD_DIM = 128
MEM_W = MEM_HEADS * MEM_HEAD_DIM

EV_BG, EV_CG, EV_XV, EV_GA, EV_Q, EV_K, EV_V, EV_GB, EV_QM, EV_GM = (
    0, 1024, 2048, 3072, 4096, 5120, 6144, 7168, 8192, 8704)
OD_XR, OD_GC, OD_U, OD_V, OD_GD, OD_QM, OD_GM = (0, 1024, 2048, 3072, 4096, 5120, 5632)

V7X_VMEM_LIMIT_BYTES = 60 * 1024 * 1024
SUBLANES = 8
MXU_DEPTH = 256
HALO = 16
TOKEN_TILE = 512
NA_ROW_BLOCK = 8
NA_PAIRS = NA_HEADS // 2
MASK_BIAS = -1e30
LOG2E = 1.4426950408889634

_COMPILER_PARAMS = pltpu.CompilerParams(
    dimension_semantics=("arbitrary", "arbitrary"), vmem_limit_bytes=V7X_VMEM_LIMIT_BYTES)


def _dot(a, b):
    return jnp.dot(a, b, preferred_element_type=F32)


def _dot_nt(a, b):
    return lax.dot_general(a, b, (((1,), (1,)), ((), ())), preferred_element_type=F32)


def _rms(x, g):
    return x * lax.rsqrt(jnp.mean(x * x, axis=-1, keepdims=True) + EPS) * g


def _sigmoid(x):
    return 1.0 / (1.0 + jnp.exp(-x))


def _silu(x):
    return x * _sigmoid(x)


def _mem_attention_head(qm, gm, kv_ref, out_ref, hh):
    scale = MEM_HEAD_DIM ** -0.5 * LOG2E
    lo, hi = hh * MEM_HEAD_DIM, (hh + 1) * MEM_HEAD_DIM
    s = _dot_nt(qm.astype(BF16), kv_ref[0, :, lo:hi]) * scale
    e = jnp.exp2(s - jnp.max(s, axis=-1, keepdims=True))
    o = _dot(e.astype(BF16), kv_ref[0, :, MEM_W + lo:MEM_W + hi])
    o = o / jnp.sum(e, axis=-1, keepdims=True)
    out_ref[0, :, lo:hi] = (o * _silu(gm)).astype(out_ref.dtype)


def _mem_attention(qm, gm, kv_ref, out_ref):
    for hh in range(MEM_HEADS):
        lo, hi = hh * MEM_HEAD_DIM, (hh + 1) * MEM_HEAD_DIM
        _mem_attention_head(qm[:, lo:hi], gm[:, lo:hi], kv_ref, out_ref, hh)


def _normed_with_halo(x_ref, xp_ref, xn_ref, g):
    h = _rms(x_ref[0], g).astype(BF16)
    hp = _rms(xp_ref[0], g).astype(BF16)
    hn = _rms(xn_ref[0], g).astype(BF16)
    return h, jnp.concatenate([hp, h, hn], axis=0)


def _halo_valid(tt):
    i = pl.program_id(1)
    last = pl.num_programs(1) - 1
    row = lax.broadcasted_iota(jnp.int32, (tt + 2 * HALO, 1), 0)
    return ((row >= HALO) | (i > 0)) & ((row < tt + HALO) | (i < last))


def _memkv_kernel(mem_ref, g_ref, w_ref, kv_ref):
    hn = _rms(mem_ref[...], g_ref[0]).astype(BF16)
    kv_ref[0] = _dot(hn, w_ref[0]).astype(BF16)


def _memkv(mem_all, mem_norm_g, w_mem_kv):
    depth = w_mem_kv.shape[0]
    nb = mem_all.shape[0] // N_MEM
    return pl.pallas_call(
        _memkv_kernel,
        grid=(depth, nb),
        in_specs=[
            pl.BlockSpec((N_MEM, D_MODEL), lambda l, b: (b, 0)),
            pl.BlockSpec((1, 1, D_MODEL), lambda l, b: (l, 0, 0)),
            pl.BlockSpec((1, D_MODEL, 2 * MEM_W), lambda l, b: (l, 0, 0)),
        ],
        out_specs=pl.BlockSpec((1, N_MEM, 2 * MEM_W), lambda l, b: (l, b, 0)),
        out_shape=jax.ShapeDtypeStruct((depth, nb * N_MEM, 2 * MEM_W), BF16),
        compiler_params=_COMPILER_PARAMS,
        name="memkv",
    )(mem_all, mem_norm_g.reshape(depth, 1, D_MODEL), w_mem_kv)


def _even_front_kernel(x_ref, xp_ref, xn_ref, g_ref, w_ref, cw_ref, cb_ref, kv_ref,
                       ya_ref, q_ref, k_ref, v_ref, sgb_ref, ym_ref, p_scr):
    tt = x_ref.shape[1]
    h, h_ext = _normed_with_halo(x_ref, xp_ref, xn_ref, g_ref[...])

    cg = _dot(h_ext, w_ref[:, EV_CG:EV_CG + D_MODEL])
    xv = _dot(h_ext, w_ref[:, EV_XV:EV_XV + D_MODEL])
    p_scr[...] = jnp.where(_halo_valid(tt), cg * xv, 0.0)
    conv = cb_ref[...]
    for j in range(CONV_K):
        conv = conv + cw_ref[j:j + 1, :] * p_scr[pl.ds(HALO - CONV_K // 2 + j, tt), :]
    bg = _dot(h, w_ref[:, EV_BG:EV_BG + D_MODEL])
    ga = _dot(h, w_ref[:, EV_GA:EV_GA + D_MODEL])
    ya_ref[0] = (bg * conv * _silu(ga)).astype(BF16)

    q_ref[0] = (_dot(h, w_ref[:, EV_Q:EV_Q + D_MODEL]) * (NA_HEAD_DIM ** -0.5 * LOG2E)).astype(BF16)
    k_ref[0] = _dot(h, w_ref[:, EV_K:EV_K + D_MODEL]).astype(BF16)
    v_ref[0] = _dot(h, w_ref[:, EV_V:EV_V + D_MODEL]).astype(BF16)
    sgb_ref[0] = _silu(_dot(h, w_ref[:, EV_GB:EV_GB + D_MODEL])).astype(BF16)

    qm = _dot(h, w_ref[:, EV_QM:EV_QM + MEM_W])
    gm = _dot(h, w_ref[:, EV_GM:EV_GM + MEM_W])
    _mem_attention(qm, gm, kv_ref, ym_ref)


def _tile_specs(tt, t_len):
    per = tt // HALO
    n_halo = t_len // HALO
    return [
        pl.BlockSpec((1, tt, D_MODEL), lambda b, i: (b, i, 0)),
        pl.BlockSpec((1, HALO, D_MODEL), lambda b, i: (b, jnp.maximum(i * per - 1, 0), 0)),
        pl.BlockSpec((1, HALO, D_MODEL), lambda b, i: (b, jnp.minimum((i + 1) * per, n_halo - 1), 0)),
    ]


def _const_spec(shape):
    nd = len(shape)
    return pl.BlockSpec(shape, lambda b, i: (0,) * nd, pipeline_mode=pl.Buffered(1))


def _even_front(x, g, w_in, conv_w, conv_b, kv):
    bsz, t_len, _ = x.shape
    tt = min(TOKEN_TILE, t_len)
    act = lambda width: pl.BlockSpec((1, tt, width), lambda b, i: (b, i, 0))
    out_sd = lambda width: jax.ShapeDtypeStruct((bsz, t_len, width), BF16)
    return pl.pallas_call(
        _even_front_kernel,
        grid=(bsz, t_len // tt),
        in_specs=_tile_specs(tt, t_len) + [
            _const_spec((1, D_MODEL)),
            _const_spec(w_in.shape),
            _const_spec(conv_w.shape),
            _const_spec((1, D_MODEL)),
            pl.BlockSpec((1, N_MEM, 2 * MEM_W), lambda b, i: (b, 0, 0)),
        ],
        out_specs=[act(D_MODEL)] * 5 + [act(MEM_W)],
        out_shape=[out_sd(D_MODEL)] * 5 + [out_sd(MEM_W)],
        scratch_shapes=[pltpu.VMEM((tt + 2 * HALO, D_MODEL), F32)],
        compiler_params=_COMPILER_PARAMS,
        name="even_front",
    )(x, x, x, g.reshape(1, D_MODEL), w_in, conv_w, conv_b.reshape(1, D_MODEL), kv)


def _na_bias_table(rpb):
    col = np.arange(GRID_W)
    cstart = np.clip(col - NA_WIN_COLS // 2, 0, GRID_W - NA_WIN_COLS)
    col_ok = (col[None, :] >= cstart[:, None]) & (col[None, :] < cstart[:, None] + NA_WIN_COLS)
    dc_idx = np.clip(col[None, :] - col[:, None], -(NA_WIN_COLS - 1), NA_WIN_COLS - 1) + NA_WIN_COLS - 1
    select = ((dc_idx[None] == np.arange(2 * NA_WIN_COLS - 1)[:, None, None]) & col_ok[None]).astype(np.float32)
    mask = np.where(col_ok, 0.0, MASK_BIAS).astype(np.float32)
    scaled = rpb.astype(F32) * LOG2E
    rows2 = jnp.stack([scaled[:, :-1], scaled[:, 1:]], axis=2)
    two = jnp.einsum("hdwm,mck->dhcwk", rows2, select, precision=lax.Precision.HIGHEST)
    two = two + mask[None, None, :, None, :]
    return two.reshape(2 * NA_WIN_ROWS - 2, NA_PAIRS, 2 * GRID_W, 2 * GRID_W)


def _even_back_kernel(q_ref, k0_ref, kn_ref, v0_ref, vn_ref, sgb_ref, ya_ref, ym_ref,
                      x_ref, u_ref, w_ref, out_ref, kwin, vwin, yb_scr, s_scr, m_scr):
    rbt = q_ref.shape[1]
    rb = rbt // GRID_W
    j = pl.program_id(1)
    rows = pl.num_programs(1) * rb
    win_keys = NA_WIN_ROWS * GRID_W

    @pl.when(j == 0)
    def _():
        kwin[rbt:2 * rbt] = k0_ref[0]
        vwin[rbt:2 * rbt] = v0_ref[0]

    @pl.when(j > 0)
    def _():
        kwin[0:2 * rbt] = kwin[rbt:3 * rbt]
        vwin[0:2 * rbt] = vwin[rbt:3 * rbt]

    kwin[2 * rbt:3 * rbt] = kn_ref[0]
    vwin[2 * rbt:3 * rbt] = vn_ref[0]

    lane = lax.broadcasted_iota(jnp.int32, (GRID_W, 2 * NA_HEAD_DIM), 1)
    first_head = lane < NA_HEAD_DIM

    independent = [(ya_ref, c, c) for c in range(0, D_MODEL, MXU_DEPTH)]
    independent += [(ym_ref, c, 2 * D_MODEL + c) for c in range(0, MEM_W, MXU_DEPTH)]
    out_ref[0] = x_ref[0]

    for i in range(rb):
        r = j * rb + i
        rstart = jnp.clip(r - NA_WIN_ROWS // 2, 0, rows - NA_WIN_ROWS)
        local = pl.multiple_of((rstart - (j - 1) * rb) * GRID_W, GRID_W)
        off = rstart - r + NA_WIN_ROWS - 1
        qrows = slice(i * GRID_W, (i + 1) * GRID_W)
        for p in range(NA_PAIRS):
            lo, hi = p * 2 * NA_HEAD_DIM, (p + 1) * 2 * NA_HEAD_DIM
            qp = q_ref[0, qrows, lo:hi]
            zero = jnp.zeros_like(qp)
            q2 = jnp.concatenate([jnp.where(first_head, qp, zero), jnp.where(first_head, zero, qp)], axis=0)
            s = _dot_nt(q2, kwin[pl.ds(local, win_keys), lo:hi])
            bias = jnp.concatenate([u_ref[off + 2 * m, p] for m in range(NA_WIN_ROWS // 2)], axis=1)
            s = s + bias
            s_scr[p] = s
            m_scr[p] = jnp.broadcast_to(jnp.max(s, axis=-1, keepdims=True), (2 * GRID_W, 2 * NA_HEAD_DIM))
        if independent:
            ref, c, wrow = independent.pop(0)
            out_ref[0] += _dot(ref[0, :, c:c + MXU_DEPTH], w_ref[wrow:wrow + MXU_DEPTH, :])
        for p in range(NA_PAIRS):
            lo, hi = p * 2 * NA_HEAD_DIM, (p + 1) * 2 * NA_HEAD_DIM
            m = m_scr[p]
            e = jnp.concatenate(
                [jnp.exp2(s_scr[p, :, c * 128:(c + 1) * 128] - m) for c in range(win_keys // 128)], axis=1)
            o2 = _dot(e.astype(BF16), vwin[pl.ds(local, win_keys), lo:hi])
            o2 = o2 / jnp.sum(e, axis=-1, keepdims=True)
            o = jnp.where(first_head, o2[:GRID_W], o2[GRID_W:])
            gate = sgb_ref[0, qrows, lo:hi].astype(F32)
            yb_scr[qrows, lo:hi] = (o * gate).astype(BF16)
    assert not independent

    out_ref[0] += _dot(yb_scr[...], w_ref[D_MODEL:2 * D_MODEL, :])


def _even_back(x, q, k, v, sgb, ya, ym, ubias, w_out):
    bsz, t_len, _ = x.shape
    rbt = NA_ROW_BLOCK * GRID_W
    nblk = t_len // rbt
    cur = lambda width: pl.BlockSpec((1, rbt, width), lambda b, j: (b, j, 0))
    first = pl.BlockSpec((1, rbt, D_MODEL), lambda b, j: (b, 0, 0))
    nxt = pl.BlockSpec((1, rbt, D_MODEL), lambda b, j: (b, jnp.minimum(j + 1, nblk - 1), 0))
    return pl.pallas_call(
        _even_back_kernel,
        grid=(bsz, nblk),
        in_specs=[cur(D_MODEL), first, nxt, first, nxt,
                  cur(D_MODEL), cur(D_MODEL), cur(MEM_W), cur(D_MODEL),
                  _const_spec(ubias.shape), _const_spec(w_out.shape)],
        out_specs=cur(D_MODEL),
        out_shape=jax.ShapeDtypeStruct((bsz, t_len, D_MODEL), F32),
        scratch_shapes=[pltpu.VMEM((3 * rbt, D_MODEL), BF16), pltpu.VMEM((3 * rbt, D_MODEL), BF16),
                        pltpu.VMEM((rbt, D_MODEL), BF16),
                        pltpu.VMEM((NA_PAIRS, 2 * GRID_W, NA_WIN_ROWS * GRID_W), F32),
                        pltpu.VMEM((NA_PAIRS, 2 * GRID_W, 2 * NA_HEAD_DIM), F32)],
        compiler_params=_COMPILER_PARAMS,
        name="even_back",
    )(q, k, k, v, v, sgb, ya, ym, x, ubias, w_out)


def _lru_gates(xc, wg_ref, n):
    return _dot(xc.astype(BF16), wg_ref[n])


def _lru_coeffs(xc, pre, ba, bi, log2_a_scale):
    r = _sigmoid(pre[:, :LRU_BW] + ba)
    gate_i = _sigmoid(pre[:, LRU_BW:] + bi)
    a = jnp.exp2(r * log2_a_scale)
    return a, jnp.sqrt(1.0 - a * a) * (gate_i * xc)


def _softplus_neg(lam):
    return jnp.maximum(-lam, 0.0) + jnp.log(1.0 + jnp.exp(-jnp.abs(lam)))


def _segment_order(tt):
    p = jnp.arange(tt)
    t = (p % SUBLANES) * (tt // SUBLANES) + p // SUBLANES
    return (t[:, None] == jnp.arange(tt)[None, :]).astype(BF16)


def _scan_sublanes(a, b, reverse):
    pos = lax.broadcasted_iota(jnp.int32, a.shape, 0)
    k = 1
    while k < SUBLANES:
        ok = (pos < SUBLANES - k) if reverse else (pos >= k)
        shift = SUBLANES - k if reverse else k
        a_s = jnp.where(ok, pltpu.roll(a, shift, 0), 1.0)
        b_s = jnp.where(ok, pltpu.roll(b, shift, 0), 0.0)
        b = a * b_s + b
        a = a * a_s
        k *= 2
    return a, b


def _lru_scan(a, b, carry, reverse):
    tt, width = a.shape
    seg = tt // SUBLANES
    a = a.reshape(seg, SUBLANES, width)
    b = b.reshape(seg, SUBLANES, width)
    steps = range(seg - 1, -1, -1) if reverse else range(seg)
    a_tot, h_tot = None, None
    for j in steps:
        if a_tot is None:
            a_tot, h_tot = a[j], b[j]
        else:
            a_tot, h_tot = a_tot * a[j], a[j] * h_tot + b[j]
    a_cum, h_cum = _scan_sublanes(a_tot, h_tot, reverse)
    leaving = a_cum * carry + h_cum
    pos = lax.broadcasted_iota(jnp.int32, leaving.shape, 0)
    if reverse:
        h = jnp.where(pos == SUBLANES - 1, carry, pltpu.roll(leaving, SUBLANES - 1, 0))
        carry = leaving[0:1, :]
    else:
        h = jnp.where(pos == 0, carry, pltpu.roll(leaving, 1, 0))
        carry = leaving[SUBLANES - 1:SUBLANES, :]
    hs = [None] * seg
    for j in steps:
        h = a[j] * h + b[j]
        hs[j] = h
    return jnp.concatenate(hs, axis=0), carry


def _odd_front_kernel(x_ref, xp_ref, xn_ref, g_ref, w_ref, cw_ref, cb_ref, wg_ref, ba_ref, bi_ref,
                      lam_ref, sg_ref, sw_ref, sbt_ref, kv_ref, perm_ref,
                      xc_ref, hf_ref, sgc_ref, yd_ref, ym_ref, xr_scr, carry_scr, z_scr, vn_scr):
    tt = x_ref.shape[1]
    seg = tt // SUBLANES
    i = pl.program_id(1)
    h, h_ext = _normed_with_halo(x_ref, xp_ref, xn_ref, g_ref[...])
    hs = _dot(perm_ref[...], h).astype(BF16)

    @pl.when(i == 0)
    def _():
        carry_scr[...] = jnp.zeros_like(carry_scr)

    lhs = jnp.concatenate([hs, h_ext[:HALO], h_ext[HALO + tt:]], axis=0)
    xr = _dot(lhs, w_ref[:, OD_XR:OD_XR + D_MODEL])

    def project(col):
        if col < OD_U:
            z = _dot(hs, w_ref[:, col:col + MXU_DEPTH])
            sgc_ref[0, :, col - OD_GC:col - OD_GC + MXU_DEPTH] = _silu(z).astype(BF16)
        else:
            z_scr[:, col - OD_U:col - OD_U + MXU_DEPTH] = _dot(h, w_ref[:, col:col + MXU_DEPTH])

    def mem_head(hh):
        qcol = OD_QM - OD_U + hh * MEM_HEAD_DIM
        gcol = OD_GM - OD_U + hh * MEM_HEAD_DIM
        _mem_attention_head(z_scr[:, qcol:qcol + MEM_HEAD_DIM], z_scr[:, gcol:gcol + MEM_HEAD_DIM],
                            kv_ref, ym_ref, hh)

    def norm_v():
        vn_scr[...] = _rms(z_scr[:, OD_V - OD_U:OD_GD - OD_U], sg_ref[...]).astype(BF16)

    def sgu_chunk(c):
        r0, r1 = c * SGU_CHUNK, (c + 1) * SGU_CHUNK
        for grp in range(SGU_GROUPS):
            lo, hi = grp * LRU_BW, (grp + 1) * LRU_BW
            mixed = _dot(sw_ref[grp], vn_scr[r0:r1, lo:hi]) + sbt_ref[:, grp:grp + 1]
            u = z_scr[r0:r1, lo:hi]
            sgd = _silu(z_scr[r0:r1, OD_GD - OD_U + lo:OD_GD - OD_U + hi])
            yd_ref[0, r0:r1, lo:hi] = (u * mixed * sgd).astype(BF16)

    slices = lambda start: [functools.partial(project, start + c) for c in range(0, D_MODEL, MXU_DEPTH)]
    for col in range(OD_QM, w_ref.shape[1], MXU_DEPTH):
        project(col)
    side = [
        (slices(OD_V)[:2], [functools.partial(mem_head, 0)]),
        (slices(OD_V)[2:], [functools.partial(mem_head, 1), norm_v]),
        (slices(OD_U)[:2], [functools.partial(mem_head, 2)]),
        (slices(OD_U)[2:], [functools.partial(mem_head, 3)]),
        (slices(OD_GD)[:2], []),
        (slices(OD_GD)[2:], []),
        (slices(OD_GC)[:2], [functools.partial(sgu_chunk, c) for c in range(0, tt // SGU_CHUNK, 2)]),
        (slices(OD_GC)[2:], [functools.partial(sgu_chunk, c) for c in range(1, tt // SGU_CHUNK, 2)]),
    ]
    assert len(side) == LRU_BLOCKS and MEM_HEADS == 4

    before =jnp.where(i > 0, xr[tt + HALO - 2:tt + HALO], 0.0)
    after = jnp.where(i < pl.num_programs(1) - 1, xr[tt + HALO:tt + HALO + 1], 0.0)
    sub = lax.broadcasted_iota(jnp.int32, (SUBLANES, 1), 0)
    lead = LRU_CONV_K // 2
    for d in range(lead):
        slab = pltpu.roll(xr[(seg + d - lead) * SUBLANES:(seg + d - lead + 1) * SUBLANES], 1, 0)
        xr_scr[d * SUBLANES:(d + 1) * SUBLANES] = jnp.where(sub == 0, before[d:d + 1], slab)
    xr_scr[lead * SUBLANES:lead * SUBLANES + tt] = xr[:tt]
    slab = pltpu.roll(xr[0:SUBLANES], SUBLANES - 1, 0)
    xr_scr[lead * SUBLANES + tt:] = jnp.where(sub == SUBLANES - 1, after, slab)
    xc = cb_ref[...]
    for j in range(LRU_CONV_K):
        xc = xc + cw_ref[j:j + 1, :] * xr_scr[j * SUBLANES:j * SUBLANES + tt, :]
    xc_ref[0] = xc

    log2_a_scale = (-LRU_C * LOG2E) * _softplus_neg(lam_ref[...])
    pre = _lru_gates(xc[:, 0:LRU_BW], wg_ref, 0)
    for n in range(LRU_BLOCKS):
        lo, hi = n * LRU_BW, (n + 1) * LRU_BW
        cur = pre
        if n + 1 < LRU_BLOCKS:
            pre = _lru_gates(xc[:, hi:hi + LRU_BW], wg_ref, n + 1)
        projections, consumers = side[n]
        for piece in projections:
            piece()
        a, b = _lru_coeffs(xc[:, lo:hi], cur, ba_ref[:, lo:hi], bi_ref[:, lo:hi], log2_a_scale[:, lo:hi])
        hf_ref[0, :, lo:hi], carry_scr[:, lo:hi] = _lru_scan(a, b, carry_scr[:, lo:hi], reverse=False)
        for piece in consumers:
            piece()


def _odd_front(x, g, w_in, conv_w, conv_b, wg, ba, bi, lam, sgu_g, sgu_w, sgu_bt, kv):
    bsz, t_len, _ = x.shape
    tt = min(TOKEN_TILE, t_len)
    act = lambda width: pl.BlockSpec((1, tt, width), lambda b, i: (b, i, 0))
    sd = lambda width, dt: jax.ShapeDtypeStruct((bsz, t_len, width), dt)
    row = _const_spec((1, D_MODEL))
    return pl.pallas_call(
        _odd_front_kernel,
        grid=(bsz, t_len // tt),
        in_specs=_tile_specs(tt, t_len) + [
            row, _const_spec(w_in.shape), _const_spec(conv_w.shape), row,
            _const_spec(wg.shape), row, row, row,
            row, _const_spec(sgu_w.shape), _const_spec(sgu_bt.shape),
            pl.BlockSpec((1, N_MEM, 2 * MEM_W), lambda b, i: (b, 0, 0)),
            _const_spec((tt, tt)),
        ],
        out_specs=[act(D_MODEL)] * 4 + [act(MEM_W)],
        out_shape=[sd(D_MODEL, F32), sd(D_MODEL, F32), sd(D_MODEL, BF16), sd(D_MODEL, BF16), sd(MEM_W, BF16)],
        scratch_shapes=[pltpu.VMEM((tt + (LRU_CONV_K - 1) * SUBLANES, D_MODEL), F32),
                        pltpu.VMEM((1, D_MODEL), F32),
                        pltpu.VMEM((tt, w_in.shape[1] - OD_U), F32),
                        pltpu.VMEM((tt, D_MODEL), BF16)],
        compiler_params=_COMPILER_PARAMS,
        name="odd_front",
    )(x, x, x, g.reshape(1, D_MODEL), w_in, conv_w, conv_b.reshape(1, D_MODEL), wg,
      ba.reshape(1, D_MODEL), bi.reshape(1, D_MODEL), lam.reshape(1, D_MODEL),
      sgu_g.reshape(1, D_MODEL), sgu_w, sgu_bt, kv, _segment_order(tt))


def _odd_back_kernel(xc_ref, hf_ref, sgc_ref, yd_ref, ym_ref, x_ref, wg_ref, ba_ref, bi_ref, lam_ref,
                     w_ref, fg_ref, unperm_ref, out_ref, carry_scr):
    @pl.when(pl.program_id(1) == 0)
    def _():
        carry_scr[...] = jnp.zeros_like(carry_scr)

    independent = [(yd_ref, c, D_MODEL + c) for c in range(0, D_MODEL, MXU_DEPTH)]
    independent += [(ym_ref, c, 2 * D_MODEL + c) for c in range(0, MEM_W, MXU_DEPTH)]
    acc = x_ref[0]
    log2_a_scale = (-LRU_C * LOG2E) * _softplus_neg(lam_ref[...])
    yc = []
    pre = _lru_gates(xc_ref[0, :, 0:LRU_BW], wg_ref, 0)
    for n in range(LRU_BLOCKS):
        lo, hi = n * LRU_BW, (n + 1) * LRU_BW
        cur = pre
        if n + 1 < LRU_BLOCKS:
            pre = _lru_gates(xc_ref[0, :, hi:hi + LRU_BW], wg_ref, n + 1)
        if independent:
            ref, c, wrow = independent.pop(0)
            acc = acc + _dot(ref[0, :, c:c + MXU_DEPTH], w_ref[wrow:wrow + MXU_DEPTH, :])
        a, b = _lru_coeffs(xc_ref[0, :, lo:hi], cur, ba_ref[:, lo:hi], bi_ref[:, lo:hi], log2_a_scale[:, lo:hi])
        hb, carry_scr[:, lo:hi] = _lru_scan(a, b, carry_scr[:, lo:hi], reverse=True)
        yc.append(((hf_ref[0, :, lo:hi] + hb) * sgc_ref[0, :, lo:hi].astype(F32)).astype(BF16))
        if n % 2 == 1:
            yc_time = _dot(unperm_ref[...], jnp.concatenate(yc, axis=1)).astype(BF16)
            acc = acc + _dot(yc_time, w_ref[lo - LRU_BW:hi, :])
            yc = []
    assert not independent
    out_ref[0] = _rms(acc, fg_ref[...])


def _odd_back(x, xc, hf, sgc, yd, ym, wg, ba, bi, lam, w_out, final_g):
    bsz, t_len, _ = x.shape
    tt = min(TOKEN_TILE, t_len)
    nt = t_len // tt
    act = lambda width: pl.BlockSpec((1, tt, width), lambda b, i: (b, nt - 1 - i, 0))
    row = _const_spec((1, D_MODEL))
    return pl.pallas_call(
        _odd_back_kernel,
        grid=(bsz, nt),
        in_specs=[act(D_MODEL)] * 4 + [act(MEM_W), act(D_MODEL),
                  _const_spec(wg.shape), row, row, row,
                  _const_spec(w_out.shape), row, _const_spec((tt, tt))],
        out_specs=act(D_MODEL),
        out_shape=jax.ShapeDtypeStruct((bsz, t_len, D_MODEL), F32),
        scratch_shapes=[pltpu.VMEM((1, D_MODEL), F32)],
        compiler_params=_COMPILER_PARAMS,
        name="odd_back",
    )(xc, hf, sgc, yd, ym, x, wg, ba.reshape(1, D_MODEL), bi.reshape(1, D_MODEL),
      lam.reshape(1, D_MODEL), w_out, final_g.reshape(1, D_MODEL), _segment_order(tt).T)


def _trunk(x, kv, norm_g, ev_w_in, ev_conv_w, ev_conv_b, ubias, ev_w_out,
           od_w_in, od_conv_w, od_conv_b, od_lru_wa, od_lru_ba, od_lru_wi, od_lru_bi, od_lru_lam,
           od_sgu_g, od_sgu_w, od_sgu_bt, od_w_out, final_g):
    ya, q, k, v, sgb, ym = _even_front(x, norm_g[0], ev_w_in, ev_conv_w, ev_conv_b, kv[0])
    x = _even_back(x, q, k, v, sgb, ya, ym, ubias, ev_w_out)
    wg = jnp.concatenate([od_lru_wa, od_lru_wi], axis=-1)
    xc, hf, sgc, yd, ym = _odd_front(x, norm_g[1], od_w_in, od_conv_w, od_conv_b,
                                     wg[0], od_lru_ba[0], od_lru_bi[0], od_lru_lam[0],
                                     od_sgu_g, od_sgu_w, od_sgu_bt, kv[1])
    return _odd_back(x, xc, hf, sgc, yd, ym, wg[1], od_lru_ba[1], od_lru_bi[1],
                     od_lru_lam[1], od_w_out, final_g)


def kernel(x_prompt, x_sample, mem_prompt, mem_sample, norm_g, mem_norm_g, w_mem_kv, ev_w_in, ev_conv_w,
           ev_conv_b, ev_rpb, ev_w_out, od_w_in, od_conv_w, od_conv_b, od_lru_wa, od_lru_ba, od_lru_wi,
           od_lru_bi, od_lru_lam, od_sgu_g, od_sgu_w, od_sgu_b, od_w_out, final_g):
    depth = norm_g.shape[0]
    assert depth == 2 and ev_w_in.shape[0] == 1 and od_w_in.shape[0] == 1
    nb_p, nb_s = mem_prompt.shape[0], mem_sample.shape[0]
    mem_all = jnp.concatenate([mem_prompt.reshape(-1, D_MODEL), mem_sample.reshape(-1, D_MODEL)], axis=0)
    kv = _memkv(mem_all, mem_norm_g, w_mem_kv.astype(BF16)).reshape(depth, nb_p + nb_s, N_MEM, 2 * MEM_W)
    shared = (norm_g, ev_w_in[0].astype(BF16), ev_conv_w[0], ev_conv_b[0], _na_bias_table(ev_rpb[0]),
              ev_w_out[0].astype(BF16), od_w_in[0].astype(BF16), od_conv_w[0], od_conv_b[0],
              od_lru_wa[0].astype(BF16), od_lru_ba[0], od_lru_wi[0].astype(BF16), od_lru_bi[0], od_lru_lam[0],
              od_sgu_g[0], od_sgu_w[0].astype(BF16), od_sgu_b[0].T, od_w_out[0].astype(BF16), final_g)
    y_prompt = _trunk(x_prompt, kv[:, :nb_p], *shared)
    y_sample = _trunk(x_sample, kv[:, nb_p:], *shared)
    return (y_prompt, y_sample)
```

```python
import functools

import jax
import jax.numpy as jnp
import numpy as np
from jax import lax
from jax.experimental import pallas as pl
from jax.experimental.pallas import tpu as pltpu

F32 = jnp.float32
BF16 = jnp.bfloat16

D_MODEL = 1024
GRID_W = 64
N_MEM = 256
EPS = 1e-6
CONV_K = 3
NA_HEADS = 16
NA_HEAD_DIM = 64
NA_WIN_ROWS = 8
NA_WIN_COLS = 16
LRU_BLOCKS = 8
LRU_BW = 128
LRU_CONV_K = 4
LRU_C = 8.0
SGU_GROUPS = 8
SGU_CHUNK = 128
MEM_HEADS = 4
MEM_HEAD_DIM = 128
MEM_W = MEM_HEADS * MEM_HEAD_DIM

EV_BG, EV_CG, EV_XV, EV_GA, EV_Q, EV_K, EV_V, EV_GB, EV_QM, EV_GM = (
    0, 1024, 2048, 3072, 4096, 5120, 6144, 7168, 8192, 8704)
OD_XR, OD_GC, OD_U, OD_V, OD_GD, OD_QM, OD_GM = (0, 1024, 2048, 3072, 4096, 5120, 5632)

V7X_VMEM_LIMIT_BYTES = 60 * 1024 * 1024
SUBLANES = 8
MXU_DEPTH = 256
HALO = 16
TOKEN_TILE = 512
NA_ROW_BLOCK = 8
NA_PAIRS = NA_HEADS // 2
MASK_BIAS = -1e30
LOG2E = 1.4426950408889634

_COMPILER_PARAMS = pltpu.CompilerParams(
    dimension_semantics=("arbitrary", "arbitrary"), vmem_limit_bytes=V7X_VMEM_LIMIT_BYTES)


def _dot(a, b):
    return jnp.dot(a, b, preferred_element_type=F32)


def _dot_nt(a, b):
    return lax.dot_general(a, b, (((1,), (1,)), ((), ())), preferred_element_type=F32)


def _rms(x, g):
    return x * lax.rsqrt(jnp.mean(x * x, axis=-1, keepdims=True) + EPS) * g


def _sigmoid(x):
    return 1.0 / (1.0 + jnp.exp(-x))


def _silu(x):
    return x * _sigmoid(x)


def _mem_attention_head(qm, gm, kv_ref, out_ref, hh):
    scale = MEM_HEAD_DIM ** -0.5 * LOG2E
    lo, hi = hh * MEM_HEAD_DIM, (hh + 1) * MEM_HEAD_DIM
    s = _dot_nt(qm.astype(BF16), kv_ref[0, :, lo:hi]) * scale
    e = jnp.exp2(s - jnp.max(s, axis=-1, keepdims=True))
    o = _dot(e.astype(BF16), kv_ref[0, :, MEM_W + lo:MEM_W + hi])
    o = o / jnp.sum(e, axis=-1, keepdims=True)
    out_ref[0, :, lo:hi] = (o * _silu(gm)).astype(out_ref.dtype)


def _mem_attention(qm, gm, kv_ref, out_ref):
    for hh in range(MEM_HEADS):
        lo, hi = hh * MEM_HEAD_DIM, (hh + 1) * MEM_HEAD_DIM
        _mem_attention_head(qm[:, lo:hi], gm[:, lo:hi], kv_ref, out_ref, hh)


def _normed_with_halo(x_ref, xp_ref, xn_ref, g):
    h = _rms(x_ref[0], g).astype(BF16)
    hp = _rms(xp_ref[0], g).astype(BF16)
    hn = _rms(xn_ref[0], g).astype(BF16)
    return h, jnp.concatenate([hp, h, hn], axis=0)


def _halo_valid(tt):
    i = pl.program_id(1)
    last = pl.num_programs(1) - 1
    row = lax.broadcasted_iota(jnp.int32, (tt + 2 * HALO, 1), 0)
    return ((row >= HALO) | (i > 0)) & ((row < tt + HALO) | (i < last))


def _memkv_kernel(mem_ref, g_ref, w_ref, kv_ref):
    hn = _rms(mem_ref[...], g_ref[0]).astype(BF16)
    kv_ref[0] = _dot(hn, w_ref[0]).astype(BF16)


def _memkv(mem_all, mem_norm_g, w_mem_kv):
    depth = w_mem_kv.shape[0]
    nb = mem_all.shape[0] // N_MEM
    return pl.pallas_call(
        _memkv_kernel,
        grid=(depth, nb),
        in_specs=[
            pl.BlockSpec((N_MEM, D_MODEL), lambda l, b: (b, 0)),
            pl.BlockSpec((1, 1, D_MODEL), lambda l, b: (l, 0, 0)),
            pl.BlockSpec((1, D_MODEL, 2 * MEM_W), lambda l, b: (l, 0, 0)),
        ],
        out_specs=pl.BlockSpec((1, N_MEM, 2 * MEM_W), lambda l, b: (l, b, 0)),
        out_shape=jax.ShapeDtypeStruct((depth, nb * N_MEM, 2 * MEM_W), BF16),
        compiler_params=_COMPILER_PARAMS,
        name="memkv",
    )(mem_all, mem_norm_g.reshape(depth, 1, D_MODEL), w_mem_kv)


def _even_front_kernel(x_ref, xp_ref, xn_ref, g_ref, w_ref, cw_ref, cb_ref, kv_ref,
                       ya_ref, q_ref, k_ref, v_ref, sgb_ref, ym_ref, p_scr):
    tt = x_ref.shape[1]
    h, h_ext = _normed_with_halo(x_ref, xp_ref, xn_ref, g_ref[...])

    cg = _dot(h_ext, w_ref[:, EV_CG:EV_CG + D_MODEL])
    xv = _dot(h_ext, w_ref[:, EV_XV:EV_XV + D_MODEL])
    p_scr[...] = jnp.where(_halo_valid(tt), cg * xv, 0.0)
    conv = cb_ref[...]
    for j in range(CONV_K):
        conv = conv + cw_ref[j:j + 1, :] * p_scr[pl.ds(HALO - CONV_K // 2 + j, tt), :]
    bg = _dot(h, w_ref[:, EV_BG:EV_BG + D_MODEL])
    ga = _dot(h, w_ref[:, EV_GA:EV_GA + D_MODEL])
    ya_ref[0] = (bg * conv * _silu(ga)).astype(BF16)

    q_ref[0] = (_dot(h, w_ref[:, EV_Q:EV_Q + D_MODEL]) * (NA_HEAD_DIM ** -0.5 * LOG2E)).astype(BF16)
    k_ref[0] = _dot(h, w_ref[:, EV_K:EV_K + D_MODEL]).astype(BF16)
    v_ref[0] = _dot(h, w_ref[:, EV_V:EV_V + D_MODEL]).astype(BF16)
    sgb_ref[0] = _silu(_dot(h, w_ref[:, EV_GB:EV_GB + D_MODEL])).astype(BF16)

    qm = _dot(h, w_ref[:, EV_QM:EV_QM + MEM_W])
    gm = _dot(h, w_ref[:, EV_GM:EV_GM + MEM_W])
    _mem_attention(qm, gm, kv_ref, ym_ref)


def _tile_specs(tt, t_len):
    per = tt // HALO
    n_halo = t_len // HALO
    return [
        pl.BlockSpec((1, tt, D_MODEL), lambda b, i: (b, i, 0)),
        pl.BlockSpec((1, HALO, D_MODEL), lambda b, i: (b, jnp.maximum(i * per - 1, 0), 0)),
        pl.BlockSpec((1, HALO, D_MODEL), lambda b, i: (b, jnp.minimum((i + 1) * per, n_halo - 1), 0)),
    ]


def _const_spec(shape):
    nd = len(shape)
    return pl.BlockSpec(shape, lambda b, i: (0,) * nd, pipeline_mode=pl.Buffered(1))


def _even_front(x, g, w_in, conv_w, conv_b, kv):
    bsz, t_len, _ = x.shape
    tt = min(TOKEN_TILE, t_len)
    act = lambda width: pl.BlockSpec((1, tt, width), lambda b, i: (b, i, 0))
    out_sd = lambda width: jax.ShapeDtypeStruct((bsz, t_len, width), BF16)
    return pl.pallas_call(
        _even_front_kernel,
        grid=(bsz, t_len // tt),
        in_specs=_tile_specs(tt, t_len) + [
            _const_spec((1, D_MODEL)),
            _const_spec(w_in.shape),
            _const_spec(conv_w.shape),
            _const_spec((1, D_MODEL)),
            pl.BlockSpec((1, N_MEM, 2 * MEM_W), lambda b, i: (b, 0, 0)),
        ],
        out_specs=[act(D_MODEL)] * 5 + [act(MEM_W)],
        out_shape=[out_sd(D_MODEL)] * 5 + [out_sd(MEM_W)],
        scratch_shapes=[pltpu.VMEM((tt + 2 * HALO, D_MODEL), F32)],
        compiler_params=_COMPILER_PARAMS,
        name="even_front",
    )(x, x, x, g.reshape(1, D_MODEL), w_in, conv_w, conv_b.reshape(1, D_MODEL), kv)


def _na_bias_table(rpb):
    col = np.arange(GRID_W)
    cstart = np.clip(col - NA_WIN_COLS // 2, 0, GRID_W - NA_WIN_COLS)
    col_ok = (col[None, :] >= cstart[:, None]) & (col[None, :] < cstart[:, None] + NA_WIN_COLS)
    dc_idx = np.clip(col[None, :] - col[:, None], -(NA_WIN_COLS - 1), NA_WIN_COLS - 1) + NA_WIN_COLS - 1
    nrel = 2 * NA_WIN_COLS - 1
    select = ((dc_idx[None] == np.arange(nrel)[:, None, None]) & col_ok[None]).astype(np.float32)
    select2 = np.zeros((2, nrel, GRID_W, 2, GRID_W), np.float32)
    select2[0, :, :, 0, :] = select
    select2[1, :, :, 1, :] = select
    select2 = select2.reshape(2 * nrel, GRID_W, 2 * GRID_W)
    mask = np.tile(np.where(col_ok, 0.0, MASK_BIAS).astype(np.float32), (1, 2))
    scaled = rpb.astype(F32) * LOG2E
    rows2 = jnp.concatenate([scaled[:, :-1], scaled[:, 1:]], axis=-1)
    two = jnp.einsum("hdm,mcl->dhcl", rows2, select2, precision=lax.Precision.HIGHEST) + mask
    return two.reshape(2 * NA_WIN_ROWS - 2, NA_PAIRS, 2 * GRID_W, 2 * GRID_W)


def _even_back_kernel(q_ref, k0_ref, kn_ref, v0_ref, vn_ref, sgb_ref, ya_ref, ym_ref,
                      x_ref, u_ref, w_ref, out_ref, kwin, vwin, yb_scr, s_scr, m_scr):
    rbt = q_ref.shape[1]
    rb = rbt // GRID_W
    j = pl.program_id(1)
    rows = pl.num_programs(1) * rb
    win_keys = NA_WIN_ROWS * GRID_W

    @pl.when(j == 0)
    def _():
        kwin[rbt:2 * rbt] = k0_ref[0]
        vwin[rbt:2 * rbt] = v0_ref[0]

    @pl.when(j > 0)
    def _():
        kwin[0:2 * rbt] = kwin[rbt:3 * rbt]
        vwin[0:2 * rbt] = vwin[rbt:3 * rbt]

    kwin[2 * rbt:3 * rbt] = kn_ref[0]
    vwin[2 * rbt:3 * rbt] = vn_ref[0]

    lane = lax.broadcasted_iota(jnp.int32, (GRID_W, 2 * NA_HEAD_DIM), 1)
    first_head = lane < NA_HEAD_DIM

    independent = [(ya_ref, c, c) for c in range(0, D_MODEL, MXU_DEPTH)]
    independent += [(ym_ref, c, 2 * D_MODEL + c) for c in range(0, MEM_W, MXU_DEPTH)]
    out_ref[0] = x_ref[0]

    for i in range(rb):
        r = j * rb + i
        rstart = jnp.clip(r - NA_WIN_ROWS // 2, 0, rows - NA_WIN_ROWS)
        local = pl.multiple_of((rstart - (j - 1) * rb) * GRID_W, GRID_W)
        off = rstart - r + NA_WIN_ROWS - 1
        qrows = slice(i * GRID_W, (i + 1) * GRID_W)
        for p in range(NA_PAIRS):
            lo, hi = p * 2 * NA_HEAD_DIM, (p + 1) * 2 * NA_HEAD_DIM
            qp = q_ref[0, qrows, lo:hi]
            zero = jnp.zeros_like(qp)
            q2 = jnp.concatenate([jnp.where(first_head, qp, zero), jnp.where(first_head, zero, qp)], axis=0)
            s = _dot_nt(q2, kwin[pl.ds(local, win_keys), lo:hi])
            bias = jnp.concatenate([u_ref[off + 2 * m, p] for m in range(NA_WIN_ROWS // 2)], axis=1)
            s = s + bias
            s_scr[p] = s
            m_scr[p] = jnp.broadcast_to(jnp.max(s, axis=-1, keepdims=True), (2 * GRID_W, 2 * NA_HEAD_DIM))
        if independent:
            ref, c, wrow = independent.pop(0)
            out_ref[0] += _dot(ref[0, :, c:c + MXU_DEPTH], w_ref[wrow:wrow + MXU_DEPTH, :])
        for p in range(NA_PAIRS):
            lo, hi = p * 2 * NA_HEAD_DIM, (p + 1) * 2 * NA_HEAD_DIM
            m = m_scr[p]
            e = jnp.concatenate(
                [jnp.exp2(s_scr[p, :, c * 128:(c + 1) * 128] - m) for c in range(win_keys // 128)], axis=1)
            o2 = _dot(e.astype(BF16), vwin[pl.ds(local, win_keys), lo:hi])
            o2 = o2 / jnp.sum(e, axis=-1, keepdims=True)
            o = jnp.where(first_head, o2[:GRID_W], o2[GRID_W:])
            gate = sgb_ref[0, qrows, lo:hi].astype(F32)
            yb_scr[qrows, lo:hi] = (o * gate).astype(BF16)
    assert not independent

    out_ref[0] += _dot(yb_scr[...], w_ref[D_MODEL:2 * D_MODEL, :])


def _even_back(x, q, k, v, sgb, ya, ym, ubias, w_out):
    bsz, t_len, _ = x.shape
    rbt = NA_ROW_BLOCK * GRID_W
    nblk = t_len // rbt
    cur = lambda width: pl.BlockSpec((1, rbt, width), lambda b, j: (b, j, 0))
    first = pl.BlockSpec((1, rbt, D_MODEL), lambda b, j: (b, 0, 0))
    nxt = pl.BlockSpec((1, rbt, D_MODEL), lambda b, j: (b, jnp.minimum(j + 1, nblk - 1), 0))
    return pl.pallas_call(
        _even_back_kernel,
        grid=(bsz, nblk),
        in_specs=[cur(D_MODEL), first, nxt, first, nxt,
                  cur(D_MODEL), cur(D_MODEL), cur(MEM_W), cur(D_MODEL),
                  _const_spec(ubias.shape), _const_spec(w_out.shape)],
        out_specs=cur(D_MODEL),
        out_shape=jax.ShapeDtypeStruct((bsz, t_len, D_MODEL), F32),
        scratch_shapes=[pltpu.VMEM((3 * rbt, D_MODEL), BF16), pltpu.VMEM((3 * rbt, D_MODEL), BF16),
                        pltpu.VMEM((rbt, D_MODEL), BF16),
                        pltpu.VMEM((NA_PAIRS, 2 * GRID_W, NA_WIN_ROWS * GRID_W), F32),
                        pltpu.VMEM((NA_PAIRS, 2 * GRID_W, 2 * NA_HEAD_DIM), F32)],
        compiler_params=_COMPILER_PARAMS,
        name="even_back",
    )(q, k, k, v, v, sgb, ya, ym, x, ubias, w_out)


def _lru_gates(xc, wg_ref, n):
    return _dot(xc.astype(BF16), wg_ref[n])


def _lru_coeffs(xc, pre, ba, bi, log2_a_scale):
    r = _sigmoid(pre[:, :LRU_BW] + ba)
    gate_i = _sigmoid(pre[:, LRU_BW:] + bi)
    a = jnp.exp2(r * log2_a_scale)
    return a, jnp.sqrt(1.0 - a * a) * (gate_i * xc)


def _softplus_neg(lam):
    return jnp.maximum(-lam, 0.0) + jnp.log(1.0 + jnp.exp(-jnp.abs(lam)))


def _segment_order(tt):
    p = jnp.arange(tt)
    t = (p % SUBLANES) * (tt // SUBLANES) + p // SUBLANES
    return (t[:, None] == jnp.arange(tt)[None, :]).astype(BF16)


def _scan_sublanes(a, b, reverse):
    pos = lax.broadcasted_iota(jnp.int32, a.shape, 0)
    k = 1
    while k < SUBLANES:
        ok = (pos < SUBLANES - k) if reverse else (pos >= k)
        shift = SUBLANES - k if reverse else k
        a_s = jnp.where(ok, pltpu.roll(a, shift, 0), 1.0)
        b_s = jnp.where(ok, pltpu.roll(b, shift, 0), 0.0)
        b = a * b_s + b
        a = a * a_s
        k *= 2
    return a, b


def _lru_scan(a, b, carry, reverse):
    tt, width = a.shape
    seg = tt // SUBLANES
    a = a.reshape(seg, SUBLANES, width)
    b = b.reshape(seg, SUBLANES, width)
    steps = range(seg - 1, -1, -1) if reverse else range(seg)
    a_tot, h_tot = None, None
    for j in steps:
        if a_tot is None:
            a_tot, h_tot = a[j], b[j]
        else:
            a_tot, h_tot = a_tot * a[j], a[j] * h_tot + b[j]
    a_cum, h_cum = _scan_sublanes(a_tot, h_tot, reverse)
    leaving = a_cum * carry + h_cum
    pos = lax.broadcasted_iota(jnp.int32, leaving.shape, 0)
    if reverse:
        h = jnp.where(pos == SUBLANES - 1, carry, pltpu.roll(leaving, SUBLANES - 1, 0))
        carry = leaving[0:1, :]
    else:
        h = jnp.where(pos == 0, carry, pltpu.roll(leaving, 1, 0))
        carry = leaving[SUBLANES - 1:SUBLANES, :]
    hs = [None] * seg
    for j in steps:
        h = a[j] * h + b[j]
        hs[j] = h
    return jnp.concatenate(hs, axis=0), carry


def _odd_front_kernel(x_ref, xp_ref, xn_ref, g_ref, w_ref, cw_ref, cb_ref, wg_ref, ba_ref, bi_ref,
                      lam_ref, sg_ref, sw_ref, sbt_ref, kv_ref, perm_ref,
                      xc_ref, hf_ref, sgc_ref, yd_ref, ym_ref, xr_scr, carry_scr, z_scr, vn_scr):
    tt = x_ref.shape[1]
    seg = tt // SUBLANES
    i = pl.program_id(1)
    h, h_ext = _normed_with_halo(x_ref, xp_ref, xn_ref, g_ref[...])
    hs = _dot(perm_ref[...], h).astype(BF16)

    @pl.when(i == 0)
    def _():
        carry_scr[...] = jnp.zeros_like(carry_scr)

    lhs = jnp.concatenate([hs, h_ext[:HALO], h_ext[HALO + tt:]], axis=0)
    xr = _dot(lhs, w_ref[:, OD_XR:OD_XR + D_MODEL])

    def project(col):
        if col < OD_U:
            z = _dot(hs, w_ref[:, col:col + MXU_DEPTH])
            sgc_ref[0, :, col - OD_GC:col - OD_GC + MXU_DEPTH] = _silu(z).astype(BF16)
        else:
            z_scr[:, col - OD_U:col - OD_U + MXU_DEPTH] = _dot(h, w_ref[:, col:col + MXU_DEPTH])

    def mem_head(hh):
        qcol = OD_QM - OD_U + hh * MEM_HEAD_DIM
        gcol = OD_GM - OD_U + hh * MEM_HEAD_DIM
        _mem_attention_head(z_scr[:, qcol:qcol + MEM_HEAD_DIM], z_scr[:, gcol:gcol + MEM_HEAD_DIM],
                            kv_ref, ym_ref, hh)

    def norm_v():
        vn_scr[...] = _rms(z_scr[:, OD_V - OD_U:OD_GD - OD_U], sg_ref[...]).astype(BF16)

    def sgu_group(grp):
        lo, hi = grp * LRU_BW, (grp + 1) * LRU_BW
        chunks = range(tt // SGU_CHUNK)
        vn = jnp.concatenate([vn_scr[c * SGU_CHUNK:(c + 1) * SGU_CHUNK, lo:hi] for c in chunks], axis=1)
        mixed_all = _dot(sw_ref[grp], vn)
        for c in chunks:
            r0, r1 = c * SGU_CHUNK, (c + 1) * SGU_CHUNK
            mixed = mixed_all[:, c * LRU_BW:(c + 1) * LRU_BW] + sbt_ref[:, grp:grp + 1]
            u = z_scr[r0:r1, lo:hi]
            sgd = _silu(z_scr[r0:r1, OD_GD - OD_U + lo:OD_GD - OD_U + hi])
            yd_ref[0, r0:r1, lo:hi] = (u * mixed * sgd).astype(BF16)

    slices = lambda start: [functools.partial(project, start + c) for c in range(0, D_MODEL, MXU_DEPTH)]
    for col in range(OD_QM, w_ref.shape[1], MXU_DEPTH):
        project(col)
    side = [
        (slices(OD_V)[:2], [functools.partial(mem_head, 0)]),
        (slices(OD_V)[2:], [functools.partial(mem_head, 1), norm_v]),
        (slices(OD_U)[:2], [functools.partial(mem_head, 2)]),
        (slices(OD_U)[2:], [functools.partial(mem_head, 3)]),
        (slices(OD_GD)[:2], []),
        (slices(OD_GD)[2:], []),
        (slices(OD_GC)[:2], [functools.partial(sgu_group, g) for g in range(0, SGU_GROUPS // 2)]),
        (slices(OD_GC)[2:], [functools.partial(sgu_group, g) for g in range(SGU_GROUPS // 2, SGU_GROUPS)]),
    ]
    assert len(side) == LRU_BLOCKS and MEM_HEADS == 4

    before =jnp.where(i > 0, xr[tt + HALO - 2:tt + HALO], 0.0)
    after = jnp.where(i < pl.num_programs(1) - 1, xr[tt + HALO:tt + HALO + 1], 0.0)
    sub = lax.broadcasted_iota(jnp.int32, (SUBLANES, 1), 0)
    lead = LRU_CONV_K // 2
    for d in range(lead):
        slab = pltpu.roll(xr[(seg + d - lead) * SUBLANES:(seg + d - lead + 1) * SUBLANES], 1, 0)
        xr_scr[d * SUBLANES:(d + 1) * SUBLANES] = jnp.where(sub == 0, before[d:d + 1], slab)
    xr_scr[lead * SUBLANES:lead * SUBLANES + tt] = xr[:tt]
    slab = pltpu.roll(xr[0:SUBLANES], SUBLANES - 1, 0)
    xr_scr[lead * SUBLANES + tt:] = jnp.where(sub == SUBLANES - 1, after, slab)
    xc = cb_ref[...]
    for j in range(LRU_CONV_K):
        xc = xc + cw_ref[j:j + 1, :] * xr_scr[j * SUBLANES:j * SUBLANES + tt, :]
    xc_ref[0] = xc

    log2_a_scale = (-LRU_C * LOG2E) * _softplus_neg(lam_ref[...])
    pre = _lru_gates(xc[:, 0:LRU_BW], wg_ref, 0)
    for n in range(LRU_BLOCKS):
        lo, hi = n * LRU_BW, (n + 1) * LRU_BW
        cur = pre
        if n + 1 < LRU_BLOCKS:
            pre = _lru_gates(xc[:, hi:hi + LRU_BW], wg_ref, n + 1)
        projections, consumers = side[n]
        for piece in projections:
            piece()
        a, b = _lru_coeffs(xc[:, lo:hi], cur, ba_ref[:, lo:hi], bi_ref[:, lo:hi], log2_a_scale[:, lo:hi])
        hf_ref[0, :, lo:hi], carry_scr[:, lo:hi] = _lru_scan(a, b, carry_scr[:, lo:hi], reverse=False)
        for piece in consumers:
            piece()


def _odd_front(x, g, w_in, conv_w, conv_b, wg, ba, bi, lam, sgu_g, sgu_w, sgu_bt, kv):
    bsz, t_len, _ = x.shape
    tt = min(TOKEN_TILE, t_len)
    act = lambda width: pl.BlockSpec((1, tt, width), lambda b, i: (b, i, 0))
    sd = lambda width, dt: jax.ShapeDtypeStruct((bsz, t_len, width), dt)
    row = _const_spec((1, D_MODEL))
    return pl.pallas_call(
        _odd_front_kernel,
        grid=(bsz, t_len // tt),
        in_specs=_tile_specs(tt, t_len) + [
            row, _const_spec(w_in.shape), _const_spec(conv_w.shape), row,
            _const_spec(wg.shape), row, row, row,
            row, _const_spec(sgu_w.shape), _const_spec(sgu_bt.shape),
            pl.BlockSpec((1, N_MEM, 2 * MEM_W), lambda b, i: (b, 0, 0)),
            _const_spec((tt, tt)),
        ],
        out_specs=[act(D_MODEL)] * 4 + [act(MEM_W)],
        out_shape=[sd(D_MODEL, F32), sd(D_MODEL, F32), sd(D_MODEL, BF16), sd(D_MODEL, BF16), sd(MEM_W, BF16)],
        scratch_shapes=[pltpu.VMEM((tt + (LRU_CONV_K - 1) * SUBLANES, D_MODEL), F32),
                        pltpu.VMEM((1, D_MODEL), F32),
                        pltpu.VMEM((tt, w_in.shape[1] - OD_U), F32),
                        pltpu.VMEM((tt, D_MODEL), BF16)],
        compiler_params=_COMPILER_PARAMS,
        name="odd_front",
    )(x, x, x, g.reshape(1, D_MODEL), w_in, conv_w, conv_b.reshape(1, D_MODEL), wg,
      ba.reshape(1, D_MODEL), bi.reshape(1, D_MODEL), lam.reshape(1, D_MODEL),
      sgu_g.reshape(1, D_MODEL), sgu_w, sgu_bt, kv, _segment_order(tt))


def _odd_back_kernel(xc_ref, hf_ref, sgc_ref, yd_ref, ym_ref, x_ref, wg_ref, ba_ref, bi_ref, lam_ref,
                     w_ref, fg_ref, unperm_ref, out_ref, carry_scr):
    @pl.when(pl.program_id(1) == 0)
    def _():
        carry_scr[...] = jnp.zeros_like(carry_scr)

    independent = [(yd_ref, c, D_MODEL + c) for c in range(0, D_MODEL, MXU_DEPTH)]
    independent += [(ym_ref, c, 2 * D_MODEL + c) for c in range(0, MEM_W, MXU_DEPTH)]
    acc = x_ref[0]
    log2_a_scale = (-LRU_C * LOG2E) * _softplus_neg(lam_ref[...])
    yc = []
    pre = _lru_gates(xc_ref[0, :, 0:LRU_BW], wg_ref, 0)
    for n in range(LRU_BLOCKS):
        lo, hi = n * LRU_BW, (n + 1) * LRU_BW
        cur = pre
        if n + 1 < LRU_BLOCKS:
            pre = _lru_gates(xc_ref[0, :, hi:hi + LRU_BW], wg_ref, n + 1)
        if independent:
            ref, c, wrow = independent.pop(0)
            acc = acc + _dot(ref[0, :, c:c + MXU_DEPTH], w_ref[wrow:wrow + MXU_DEPTH, :])
        a, b = _lru_coeffs(xc_ref[0, :, lo:hi], cur, ba_ref[:, lo:hi], bi_ref[:, lo:hi], log2_a_scale[:, lo:hi])
        hb, carry_scr[:, lo:hi] = _lru_scan(a, b, carry_scr[:, lo:hi], reverse=True)
        yc.append(((hf_ref[0, :, lo:hi] + hb) * sgc_ref[0, :, lo:hi].astype(F32)).astype(BF16))
        if n % 2 == 1:
            yc_time = _dot(unperm_ref[...], jnp.concatenate(yc, axis=1)).astype(BF16)
            acc = acc + _dot(yc_time, w_ref[lo - LRU_BW:hi, :])
            yc = []
    assert not independent
    out_ref[0] = _rms(acc, fg_ref[...])


def _odd_back(x, xc, hf, sgc, yd, ym, wg, ba, bi, lam, w_out, final_g):
    bsz, t_len, _ = x.shape
    tt = min(TOKEN_TILE, t_len)
    nt = t_len // tt
    act = lambda width: pl.BlockSpec((1, tt, width), lambda b, i: (b, nt - 1 - i, 0))
    row = _const_spec((1, D_MODEL))
    return pl.pallas_call(
        _odd_back_kernel,
        grid=(bsz, nt),
        in_specs=[act(D_MODEL)] * 4 + [act(MEM_W), act(D_MODEL),
                  _const_spec(wg.shape), row, row, row,
                  _const_spec(w_out.shape), row, _const_spec((tt, tt))],
        out_specs=act(D_MODEL),
        out_shape=jax.ShapeDtypeStruct((bsz, t_len, D_MODEL), F32),
        scratch_shapes=[pltpu.VMEM((1, D_MODEL), F32)],
        compiler_params=_COMPILER_PARAMS,
        name="odd_back",
    )(xc, hf, sgc, yd, ym, x, wg, ba.reshape(1, D_MODEL), bi.reshape(1, D_MODEL),
      lam.reshape(1, D_MODEL), w_out, final_g.reshape(1, D_MODEL), _segment_order(tt).T)


def _trunk(x, kv, norm_g, ev_w_in, ev_conv_w, ev_conv_b, ubias, ev_w_out,
           od_w_in, od_conv_w, od_conv_b, od_lru_wa, od_lru_ba, od_lru_wi, od_lru_bi, od_lru_lam,
           od_sgu_g, od_sgu_w, od_sgu_bt, od_w_out, final_g):
    ya, q, k, v, sgb, ym = _even_front(x, norm_g[0], ev_w_in, ev_conv_w, ev_conv_b, kv[0])
    x = _even_back(x, q, k, v, sgb, ya, ym, ubias, ev_w_out)
    wg = jnp.concatenate([od_lru_wa, od_lru_wi], axis=-1)
    xc, hf, sgc, yd, ym = _odd_front(x, norm_g[1], od_w_in, od_conv_w, od_conv_b,
                                     wg[0], od_lru_ba[0], od_lru_bi[0], od_lru_lam[0],
                                     od_sgu_g, od_sgu_w, od_sgu_bt, kv[1])
    return _odd_back(x, xc, hf, sgc, yd, ym, wg[1], od_lru_ba[1], od_lru_bi[1],
                     od_lru_lam[1], od_w_out, final_g)


def kernel(x_prompt, x_sample, mem_prompt, mem_sample, norm_g, mem_norm_g, w_mem_kv, ev_w_in, ev_conv_w,
           ev_conv_b, ev_rpb, ev_w_out, od_w_in, od_conv_w, od_conv_b, od_lru_wa, od_lru_ba, od_lru_wi,
           od_lru_bi, od_lru_lam, od_sgu_g, od_sgu_w, od_sgu_b, od_w_out, final_g):
    depth = norm_g.shape[0]
    assert depth == 2 and ev_w_in.shape[0] == 1 and od_w_in.shape[0] == 1
    nb_p, nb_s = mem_prompt.shape[0], mem_sample.shape[0]
    mem_all = jnp.concatenate([mem_prompt.reshape(-1, D_MODEL), mem_sample.reshape(-1, D_MODEL)], axis=0)
    kv = _memkv(mem_all, mem_norm_g, w_mem_kv.astype(BF16)).reshape(depth, nb_p + nb_s, N_MEM, 2 * MEM_W)
    shared = (norm_g, ev_w_in[0].astype(BF16), ev_conv_w[0], ev_conv_b[0], _na_bias_table(ev_rpb[0]),
              ev_w_out[0].astype(BF16), od_w_in[0].astype(BF16), od_conv_w[0], od_conv_b[0],
              od_lru_wa[0].astype(BF16), od_lru_ba[0], od_lru_wi[0].astype(BF16), od_lru_bi[0], od_lru_lam[0],
              od_sgu_g[0], od_sgu_w[0].astype(BF16), od_sgu_b[0].T, od_w_out[0].astype(BF16), final_g)
    y_prompt = _trunk(x_prompt, kv[:, :nb_p], *shared)
    y_sample = _trunk(x_sample, kv[:, nb_p:], *shared)
    return (y_prompt, y_sample)
```

```python
import functools

import jax
import jax.numpy as jnp
import numpy as np
from jax import lax
from jax.experimental import pallas as pl
from jax.experimental.pallas import tpu as pltpu

F32 = jnp.float32
BF16 = jnp.bfloat16

D_MODEL = 1024
GRID_W = 64
N_MEM = 256
EPS = 1e-6
CONV_K = 3
NA_HEADS = 16
NA_HEAD_DIM = 64
NA_WIN_ROWS = 8
NA_WIN_COLS = 16
LRU_BLOCKS = 8
LRU_BW = 128
LRU_CONV_K = 4
LRU_C = 8.0
SGU_GROUPS = 8
SGU_CHUNK = 128
MEM_HEADS = 4
MEM_HEAD_DIM = 128
MEM_W = MEM_HEADS * MEM_HEAD_DIM

EV_BG, EV_CG, EV_XV, EV_GA, EV_Q, EV_K, EV_V, EV_GB, EV_QM, EV_GM = (
    0, 1024, 2048, 3072, 4096, 5120, 6144, 7168, 8192, 8704)
OD_XR, OD_GC, OD_U, OD_V, OD_GD, OD_QM, OD_GM = (0, 1024, 2048, 3072, 4096, 5120, 5632)

V7X_VMEM_LIMIT_BYTES = 60 * 1024 * 1024
SUBLANES = 8
MXU_DEPTH = 256
HALO = 16
TOKEN_TILE = 512
NA_ROW_BLOCK = 8
NA_PAIRS = NA_HEADS // 2
MASK_BIAS = -1e30
LOG2E = 1.4426950408889634

_COMPILER_PARAMS = pltpu.CompilerParams(
    dimension_semantics=("arbitrary", "arbitrary"), vmem_limit_bytes=V7X_VMEM_LIMIT_BYTES)


def _dot(a, b):
    return jnp.dot(a, b, preferred_element_type=F32)


def _dot_nt(a, b):
    return lax.dot_general(a, b, (((1,), (1,)), ((), ())), preferred_element_type=F32)


def _rms(x, g):
    return x * lax.rsqrt(jnp.mean(x * x, axis=-1, keepdims=True) + EPS) * g


def _sigmoid(x):
    return 1.0 / (1.0 + jnp.exp(-x))


def _silu(x):
    return x * _sigmoid(x)


def _mem_attention_head(qm, gm, kv_ref, out_ref, hh):
    scale = MEM_HEAD_DIM ** -0.5 * LOG2E
    lo, hi = hh * MEM_HEAD_DIM, (hh + 1) * MEM_HEAD_DIM
    s = _dot_nt(qm.astype(BF16), kv_ref[0, :, lo:hi]) * scale
    e = jnp.exp2(s - jnp.max(s, axis=-1, keepdims=True))
    o = _dot(e.astype(BF16), kv_ref[0, :, MEM_W + lo:MEM_W + hi])
    o = o / jnp.sum(e, axis=-1, keepdims=True)
    out_ref[0, :, lo:hi] = (o * _silu(gm)).astype(out_ref.dtype)


def _mem_attention(qm, gm, kv_ref, out_ref, heads):
    for n, hh in enumerate(heads):
        lo, hi = n * MEM_HEAD_DIM, (n + 1) * MEM_HEAD_DIM
        _mem_attention_head(qm[:, lo:hi], gm[:, lo:hi], kv_ref, out_ref, hh)


def _normed_with_halo(x_ref, xp_ref, xn_ref, g):
    h = _rms(x_ref[0], g).astype(BF16)
    hp = _rms(xp_ref[0], g).astype(BF16)
    hn = _rms(xn_ref[0], g).astype(BF16)
    return h, jnp.concatenate([hp, h, hn], axis=0)


def _halo_valid(tt):
    i = pl.program_id(1)
    last = pl.num_programs(1) - 1
    row = lax.broadcasted_iota(jnp.int32, (tt + 2 * HALO, 1), 0)
    return ((row >= HALO) | (i > 0)) & ((row < tt + HALO) | (i < last))


def _memkv_kernel(mem_ref, g_ref, w_ref, kv_ref):
    hn = _rms(mem_ref[...], g_ref[0]).astype(BF16)
    kv_ref[0] = _dot(hn, w_ref[0]).astype(BF16)


def _memkv(mem_all, mem_norm_g, w_mem_kv):
    depth = w_mem_kv.shape[0]
    nb = mem_all.shape[0] // N_MEM
    return pl.pallas_call(
        _memkv_kernel,
        grid=(depth, nb),
        in_specs=[
            pl.BlockSpec((N_MEM, D_MODEL), lambda l, b: (b, 0)),
            pl.BlockSpec((1, 1, D_MODEL), lambda l, b: (l, 0, 0)),
            pl.BlockSpec((1, D_MODEL, 2 * MEM_W), lambda l, b: (l, 0, 0)),
        ],
        out_specs=pl.BlockSpec((1, N_MEM, 2 * MEM_W), lambda l, b: (l, b, 0)),
        out_shape=jax.ShapeDtypeStruct((depth, nb * N_MEM, 2 * MEM_W), BF16),
        compiler_params=_COMPILER_PARAMS,
        name="memkv",
    )(mem_all, mem_norm_g.reshape(depth, 1, D_MODEL), w_mem_kv)


def _even_front_kernel(x_ref, xp_ref, xn_ref, g_ref, w_ref, cw_ref, cb_ref, kv_ref,
                       ya_ref, q_ref, k_ref, v_ref, sgb_ref, ym_ref, p_scr):
    tt = x_ref.shape[1]
    h, h_ext = _normed_with_halo(x_ref, xp_ref, xn_ref, g_ref[...])

    cg = _dot(h_ext, w_ref[:, EV_CG:EV_CG + D_MODEL])
    xv = _dot(h_ext, w_ref[:, EV_XV:EV_XV + D_MODEL])
    p_scr[...] = jnp.where(_halo_valid(tt), cg * xv, 0.0)
    qm = _dot(h, w_ref[:, EV_QM:EV_QM + MEM_W])
    gm = _dot(h, w_ref[:, EV_GM:EV_GM + MEM_W])
    q_ref[0] = (_dot(h, w_ref[:, EV_Q:EV_Q + D_MODEL]) * (NA_HEAD_DIM ** -0.5 * LOG2E)).astype(BF16)
    conv = cb_ref[...]
    for j in range(CONV_K):
        conv = conv + cw_ref[j:j + 1, :] * p_scr[pl.ds(HALO - CONV_K // 2 + j, tt), :]
    k_ref[0] = _dot(h, w_ref[:, EV_K:EV_K + D_MODEL]).astype(BF16)
    _mem_attention(qm[:, :MEM_W // 2], gm[:, :MEM_W // 2], kv_ref, ym_ref, range(MEM_HEADS // 2))
    bg = _dot(h, w_ref[:, EV_BG:EV_BG + D_MODEL])
    ga = _dot(h, w_ref[:, EV_GA:EV_GA + D_MODEL])
    ya_ref[0] = (bg * conv * _silu(ga)).astype(BF16)
    sgb_ref[0] = _silu(_dot(h, w_ref[:, EV_GB:EV_GB + D_MODEL])).astype(BF16)
    _mem_attention(qm[:, MEM_W // 2:], gm[:, MEM_W // 2:], kv_ref, ym_ref, range(MEM_HEADS // 2, MEM_HEADS))
    v_ref[0] = _dot(h, w_ref[:, EV_V:EV_V + D_MODEL]).astype(BF16)


def _tile_specs(tt, t_len):
    per = tt // HALO
    n_halo = t_len // HALO
    return [
        pl.BlockSpec((1, tt, D_MODEL), lambda b, i: (b, i, 0)),
        pl.BlockSpec((1, HALO, D_MODEL), lambda b, i: (b, jnp.maximum(i * per - 1, 0), 0)),
        pl.BlockSpec((1, HALO, D_MODEL), lambda b, i: (b, jnp.minimum((i + 1) * per, n_halo - 1), 0)),
    ]


def _const_spec(shape):
    nd = len(shape)
    return pl.BlockSpec(shape, lambda b, i: (0,) * nd, pipeline_mode=pl.Buffered(1))


def _even_front(x, g, w_in, conv_w, conv_b, kv):
    bsz, t_len, _ = x.shape
    tt = min(TOKEN_TILE, t_len)
    act = lambda width: pl.BlockSpec((1, tt, width), lambda b, i: (b, i, 0))
    out_sd = lambda width: jax.ShapeDtypeStruct((bsz, t_len, width), BF16)
    return pl.pallas_call(
        _even_front_kernel,
        grid=(bsz, t_len // tt),
        in_specs=_tile_specs(tt, t_len) + [
            _const_spec((1, D_MODEL)),
            _const_spec(w_in.shape),
            _const_spec(conv_w.shape),
            _const_spec((1, D_MODEL)),
            pl.BlockSpec((1, N_MEM, 2 * MEM_W), lambda b, i: (b, 0, 0)),
        ],
        out_specs=[act(D_MODEL)] * 5 + [act(MEM_W)],
        out_shape=[out_sd(D_MODEL)] * 5 + [out_sd(MEM_W)],
        scratch_shapes=[pltpu.VMEM((tt + 2 * HALO, D_MODEL), F32)],
        compiler_params=_COMPILER_PARAMS,
        name="even_front",
    )(x, x, x, g.reshape(1, D_MODEL), w_in, conv_w, conv_b.reshape(1, D_MODEL), kv)


def _na_bias_table(rpb):
    col = np.arange(GRID_W)
    cstart = np.clip(col - NA_WIN_COLS // 2, 0, GRID_W - NA_WIN_COLS)
    col_ok = (col[None, :] >= cstart[:, None]) & (col[None, :] < cstart[:, None] + NA_WIN_COLS)
    dc_idx = np.clip(col[None, :] - col[:, None], -(NA_WIN_COLS - 1), NA_WIN_COLS - 1) + NA_WIN_COLS - 1
    nrel = 2 * NA_WIN_COLS - 1
    select = ((dc_idx[None] == np.arange(nrel)[:, None, None]) & col_ok[None]).astype(np.float32)
    select2 = np.zeros((2, nrel, GRID_W, 2, GRID_W), np.float32)
    select2[0, :, :, 0, :] = select
    select2[1, :, :, 1, :] = select
    select2 = select2.reshape(2 * nrel, GRID_W, 2 * GRID_W)
    mask = np.tile(np.where(col_ok, 0.0, MASK_BIAS).astype(np.float32), (1, 2))
    scaled = rpb.astype(F32) * LOG2E
    rows2 = jnp.concatenate([scaled[:, :-1], scaled[:, 1:]], axis=-1)
    two = jnp.einsum("hdm,mcl->dhcl", rows2, select2, precision=lax.Precision.HIGHEST) + mask
    return two.reshape(2 * NA_WIN_ROWS - 2, NA_PAIRS, 2 * GRID_W, 2 * GRID_W)


def _even_back_kernel(q_ref, k0_ref, kn_ref, v0_ref, vn_ref, sgb_ref, ya_ref, ym_ref,
                      x_ref, u_ref, w_ref, out_ref, kwin, vwin, yb_scr, s_scr, m_scr):
    rbt = q_ref.shape[1]
    rb = rbt // GRID_W
    j = pl.program_id(1)
    rows = pl.num_programs(1) * rb
    win_keys = NA_WIN_ROWS * GRID_W

    @pl.when(j == 0)
    def _():
        kwin[rbt:2 * rbt] = k0_ref[0]
        vwin[rbt:2 * rbt] = v0_ref[0]

    @pl.when(j > 0)
    def _():
        kwin[0:2 * rbt] = kwin[rbt:3 * rbt]
        vwin[0:2 * rbt] = vwin[rbt:3 * rbt]

    kwin[2 * rbt:3 * rbt] = kn_ref[0]
    vwin[2 * rbt:3 * rbt] = vn_ref[0]

    lane = lax.broadcasted_iota(jnp.int32, (GRID_W, 2 * NA_HEAD_DIM), 1)
    first_head = lane < NA_HEAD_DIM

    independent = [(ya_ref, c, c) for c in range(0, D_MODEL, MXU_DEPTH)]
    independent += [(ym_ref, c, 2 * D_MODEL + c) for c in range(0, MEM_W, MXU_DEPTH)]
    out_ref[0] = x_ref[0]

    for i in range(rb):
        r = j * rb + i
        rstart = jnp.clip(r - NA_WIN_ROWS // 2, 0, rows - NA_WIN_ROWS)
        local = pl.multiple_of((rstart - (j - 1) * rb) * GRID_W, GRID_W)
        off = rstart - r + NA_WIN_ROWS - 1
        qrows = slice(i * GRID_W, (i + 1) * GRID_W)
        for p in range(NA_PAIRS):
            lo, hi = p * 2 * NA_HEAD_DIM, (p + 1) * 2 * NA_HEAD_DIM
            qp = q_ref[0, qrows, lo:hi]
            zero = jnp.zeros_like(qp)
            q2 = jnp.concatenate([jnp.where(first_head, qp, zero), jnp.where(first_head, zero, qp)], axis=0)
            s = _dot_nt(q2, kwin[pl.ds(local, win_keys), lo:hi])
            bias = jnp.concatenate([u_ref[off + 2 * m, p] for m in range(NA_WIN_ROWS // 2)], axis=1)
            s = s + bias
            s_scr[p] = s
            m_scr[p] = jnp.broadcast_to(jnp.max(s, axis=-1, keepdims=True), (2 * GRID_W, 2 * NA_HEAD_DIM))
        if independent:
            ref, c, wrow = independent.pop(0)
            out_ref[0] += _dot(ref[0, :, c:c + MXU_DEPTH], w_ref[wrow:wrow + MXU_DEPTH, :])
        for p in range(NA_PAIRS):
            lo, hi = p * 2 * NA_HEAD_DIM, (p + 1) * 2 * NA_HEAD_DIM
            m = m_scr[p]
            e = jnp.concatenate(
                [jnp.exp2(s_scr[p, :, c * 128:(c + 1) * 128] - m) for c in range(win_keys // 128)], axis=1)
            o2 = _dot(e.astype(BF16), vwin[pl.ds(local, win_keys), lo:hi])
            o2 = o2 / jnp.sum(e, axis=-1, keepdims=True)
            o = jnp.where(first_head, o2[:GRID_W], o2[GRID_W:])
            gate = sgb_ref[0, qrows, lo:hi].astype(F32)
            yb_scr[qrows, lo:hi] = (o * gate).astype(BF16)
    assert not independent

    out_ref[0] += _dot(yb_scr[...], w_ref[D_MODEL:2 * D_MODEL, :])


def _even_back(x, q, k, v, sgb, ya, ym, ubias, w_out):
    bsz, t_len, _ = x.shape
    rbt = NA_ROW_BLOCK * GRID_W
    nblk = t_len // rbt
    cur = lambda width: pl.BlockSpec((1, rbt, width), lambda b, j: (b, j, 0))
    first = pl.BlockSpec((1, rbt, D_MODEL), lambda b, j: (b, 0, 0))
    nxt = pl.BlockSpec((1, rbt, D_MODEL), lambda b, j: (b, jnp.minimum(j + 1, nblk - 1), 0))
    return pl.pallas_call(
        _even_back_kernel,
        grid=(bsz, nblk),
        in_specs=[cur(D_MODEL), first, nxt, first, nxt,
                  cur(D_MODEL), cur(D_MODEL), cur(MEM_W), cur(D_MODEL),
                  _const_spec(ubias.shape), _const_spec(w_out.shape)],
        out_specs=cur(D_MODEL),
        out_shape=jax.ShapeDtypeStruct((bsz, t_len, D_MODEL), F32),
        scratch_shapes=[pltpu.VMEM((3 * rbt, D_MODEL), BF16), pltpu.VMEM((3 * rbt, D_MODEL), BF16),
                        pltpu.VMEM((rbt, D_MODEL), BF16),
                        pltpu.VMEM((NA_PAIRS, 2 * GRID_W, NA_WIN_ROWS * GRID_W), F32),
                        pltpu.VMEM((NA_PAIRS, 2 * GRID_W, 2 * NA_HEAD_DIM), F32)],
        compiler_params=_COMPILER_PARAMS,
        name="even_back",
    )(q, k, k, v, v, sgb, ya, ym, x, ubias, w_out)


def _lru_gates(xc, wg_ref, n):
    return _dot(xc.astype(BF16), wg_ref[n])


def _lru_coeffs(xc, pre, ba, bi, log2_a_scale):
    r = _sigmoid(pre[:, :LRU_BW] + ba)
    gate_i = _sigmoid(pre[:, LRU_BW:] + bi)
    a = jnp.exp2(r * log2_a_scale)
    return a, jnp.sqrt(1.0 - a * a) * (gate_i * xc)


def _softplus_neg(lam):
    return jnp.maximum(-lam, 0.0) + jnp.log(1.0 + jnp.exp(-jnp.abs(lam)))


def _segment_order(tt):
    p = jnp.arange(tt)
    t = (p % SUBLANES) * (tt // SUBLANES) + p // SUBLANES
    return (t[:, None] == jnp.arange(tt)[None, :]).astype(BF16)


def _scan_sublanes(a, b, reverse):
    pos = lax.broadcasted_iota(jnp.int32, a.shape, 0)
    k = 1
    while k < SUBLANES:
        ok = (pos < SUBLANES - k) if reverse else (pos >= k)
        shift = SUBLANES - k if reverse else k
        a_s = jnp.where(ok, pltpu.roll(a, shift, 0), 1.0)
        b_s = jnp.where(ok, pltpu.roll(b, shift, 0), 0.0)
        b = a * b_s + b
        a = a * a_s
        k *= 2
    return a, b


def _lru_scan(a, b, carry, reverse):
    tt, width = a.shape
    seg = tt // SUBLANES
    a = a.reshape(seg, SUBLANES, width)
    b = b.reshape(seg, SUBLANES, width)
    steps = range(seg - 1, -1, -1) if reverse else range(seg)
    a_tot, h_tot = None, None
    for j in steps:
        if a_tot is None:
            a_tot, h_tot = a[j], b[j]
        else:
            a_tot, h_tot = a_tot * a[j], a[j] * h_tot + b[j]
    a_cum, h_cum = _scan_sublanes(a_tot, h_tot, reverse)
    leaving = a_cum * carry + h_cum
    pos = lax.broadcasted_iota(jnp.int32, leaving.shape, 0)
    if reverse:
        h = jnp.where(pos == SUBLANES - 1, carry, pltpu.roll(leaving, SUBLANES - 1, 0))
        carry = leaving[0:1, :]
    else:
        h = jnp.where(pos == 0, carry, pltpu.roll(leaving, 1, 0))
        carry = leaving[SUBLANES - 1:SUBLANES, :]
    hs = [None] * seg
    for j in steps:
        h = a[j] * h + b[j]
        hs[j] = h
    return jnp.concatenate(hs, axis=0), carry


def _odd_front_kernel(x_ref, xp_ref, xn_ref, g_ref, w_ref, cw_ref, cb_ref, wg_ref, ba_ref, bi_ref,
                      lam_ref, sg_ref, sw_ref, sbt_ref, kv_ref, perm_ref,
                      xc_ref, hf_ref, sgc_ref, yd_ref, ym_ref, xr_scr, carry_scr, z_scr, vn_scr):
    tt = x_ref.shape[1]
    seg = tt // SUBLANES
    i = pl.program_id(1)
    h, h_ext = _normed_with_halo(x_ref, xp_ref, xn_ref, g_ref[...])
    hs = _dot(perm_ref[...], h).astype(BF16)

    @pl.when(i == 0)
    def _():
        carry_scr[...] = jnp.zeros_like(carry_scr)

    lhs = jnp.concatenate([hs, h_ext[:HALO], h_ext[HALO + tt:]], axis=0)
    xr = _dot(lhs, w_ref[:, OD_XR:OD_XR + D_MODEL])

    def project(col):
        if col < OD_U:
            z = _dot(hs, w_ref[:, col:col + MXU_DEPTH])
            sgc_ref[0, :, col - OD_GC:col - OD_GC + MXU_DEPTH] = _silu(z).astype(BF16)
        else:
            z_scr[:, col - OD_U:col - OD_U + MXU_DEPTH] = _dot(h, w_ref[:, col:col + MXU_DEPTH])

    def mem_head(hh):
        qcol = OD_QM - OD_U + hh * MEM_HEAD_DIM
        gcol = OD_GM - OD_U + hh * MEM_HEAD_DIM
        _mem_attention_head(z_scr[:, qcol:qcol + MEM_HEAD_DIM], z_scr[:, gcol:gcol + MEM_HEAD_DIM],
                            kv_ref, ym_ref, hh)

    def norm_v():
        vn_scr[...] = _rms(z_scr[:, OD_V - OD_U:OD_GD - OD_U], sg_ref[...]).astype(BF16)

    def sgu_group(grp):
        lo, hi = grp * LRU_BW, (grp + 1) * LRU_BW
        chunks = range(tt // SGU_CHUNK)
        vn = jnp.concatenate([vn_scr[c * SGU_CHUNK:(c + 1) * SGU_CHUNK, lo:hi] for c in chunks], axis=1)
        mixed_all = _dot(sw_ref[grp], vn)
        for c in chunks:
            r0, r1 = c * SGU_CHUNK, (c + 1) * SGU_CHUNK
            mixed = mixed_all[:, c * LRU_BW:(c + 1) * LRU_BW] + sbt_ref[:, grp:grp + 1]
            u = z_scr[r0:r1, lo:hi]
            sgd = _silu(z_scr[r0:r1, OD_GD - OD_U + lo:OD_GD - OD_U + hi])
            yd_ref[0, r0:r1, lo:hi] = (u * mixed * sgd).astype(BF16)

    slices = lambda start: [functools.partial(project, start + c) for c in range(0, D_MODEL, MXU_DEPTH)]
    for col in range(OD_QM, w_ref.shape[1], MXU_DEPTH):
        project(col)
    side = [
        (slices(OD_V)[:2], [functools.partial(mem_head, 0)]),
        (slices(OD_V)[2:], [functools.partial(mem_head, 1), norm_v]),
        (slices(OD_U)[:2], [functools.partial(mem_head, 2)]),
        (slices(OD_U)[2:], [functools.partial(mem_head, 3)]),
        (slices(OD_GD)[:2], []),
        (slices(OD_GD)[2:], []),
        (slices(OD_GC)[:2], [functools.partial(sgu_group, g) for g in range(0, SGU_GROUPS // 2)]),
        (slices(OD_GC)[2:], [functools.partial(sgu_group, g) for g in range(SGU_GROUPS // 2, SGU_GROUPS)]),
    ]
    assert len(side) == LRU_BLOCKS and MEM_HEADS == 4

    before =jnp.where(i > 0, xr[tt + HALO - 2:tt + HALO], 0.0)
    after = jnp.where(i < pl.num_programs(1) - 1, xr[tt + HALO:tt + HALO + 1], 0.0)
    sub = lax.broadcasted_iota(jnp.int32, (SUBLANES, 1), 0)
    lead = LRU_CONV_K // 2
    for d in range(lead):
        slab = pltpu.roll(xr[(seg + d - lead) * SUBLANES:(seg + d - lead + 1) * SUBLANES], 1, 0)
        xr_scr[d * SUBLANES:(d + 1) * SUBLANES] = jnp.where(sub == 0, before[d:d + 1], slab)
    xr_scr[lead * SUBLANES:lead * SUBLANES + tt] = xr[:tt]
    slab = pltpu.roll(xr[0:SUBLANES], SUBLANES - 1, 0)
    xr_scr[lead * SUBLANES + tt:] = jnp.where(sub == SUBLANES - 1, after, slab)
    xc = cb_ref[...]
    for j in range(LRU_CONV_K):
        xc = xc + cw_ref[j:j + 1, :] * xr_scr[j * SUBLANES:j * SUBLANES + tt, :]
    xc_ref[0] = xc

    log2_a_scale = (-LRU_C * LOG2E) * _softplus_neg(lam_ref[...])
    pre = _lru_gates(xc[:, 0:LRU_BW], wg_ref, 0)
    for n in range(LRU_BLOCKS):
        lo, hi = n * LRU_BW, (n + 1) * LRU_BW
        cur = pre
        if n + 1 < LRU_BLOCKS:
            pre = _lru_gates(xc[:, hi:hi + LRU_BW], wg_ref, n + 1)
        projections, consumers = side[n]
        for piece in projections:
            piece()
        a, b = _lru_coeffs(xc[:, lo:hi], cur, ba_ref[:, lo:hi], bi_ref[:, lo:hi], log2_a_scale[:, lo:hi])
        hf_ref[0, :, lo:hi], carry_scr[:, lo:hi] = _lru_scan(a, b, carry_scr[:, lo:hi], reverse=False)
        for piece in consumers:
            piece()


def _odd_front(x, g, w_in, conv_w, conv_b, wg, ba, bi, lam, sgu_g, sgu_w, sgu_bt, kv):
    bsz, t_len, _ = x.shape
    tt = min(TOKEN_TILE, t_len)
    act = lambda width: pl.BlockSpec((1, tt, width), lambda b, i: (b, i, 0))
    sd = lambda width, dt: jax.ShapeDtypeStruct((bsz, t_len, width), dt)
    row = _const_spec((1, D_MODEL))
    return pl.pallas_call(
        _odd_front_kernel,
        grid=(bsz, t_len // tt),
        in_specs=_tile_specs(tt, t_len) + [
            row, _const_spec(w_in.shape), _const_spec(conv_w.shape), row,
            _const_spec(wg.shape), row, row, row,
            row, _const_spec(sgu_w.shape), _const_spec(sgu_bt.shape),
            pl.BlockSpec((1, N_MEM, 2 * MEM_W), lambda b, i: (b, 0, 0)),
            _const_spec((tt, tt)),
        ],
        out_specs=[act(D_MODEL)] * 4 + [act(MEM_W)],
        out_shape=[sd(D_MODEL, F32), sd(D_MODEL, F32), sd(D_MODEL, BF16), sd(D_MODEL, BF16), sd(MEM_W, BF16)],
        scratch_shapes=[pltpu.VMEM((tt + (LRU_CONV_K - 1) * SUBLANES, D_MODEL), F32),
                        pltpu.VMEM((1, D_MODEL), F32),
                        pltpu.VMEM((tt, w_in.shape[1] - OD_U), F32),
                        pltpu.VMEM((tt, D_MODEL), BF16)],
        compiler_params=_COMPILER_PARAMS,
        name="odd_front",
    )(x, x, x, g.reshape(1, D_MODEL), w_in, conv_w, conv_b.reshape(1, D_MODEL), wg,
      ba.reshape(1, D_MODEL), bi.reshape(1, D_MODEL), lam.reshape(1, D_MODEL),
      sgu_g.reshape(1, D_MODEL), sgu_w, sgu_bt, kv, _segment_order(tt))


def _odd_back_kernel(xc_ref, hf_ref, sgc_ref, yd_ref, ym_ref, x_ref, wg_ref, ba_ref, bi_ref, lam_ref,
                     w_ref, fg_ref, unperm_ref, out_ref, carry_scr):
    @pl.when(pl.program_id(1) == 0)
    def _():
        carry_scr[...] = jnp.zeros_like(carry_scr)

    independent = [(yd_ref, c, D_MODEL + c) for c in range(0, D_MODEL, MXU_DEPTH)]
    independent += [(ym_ref, c, 2 * D_MODEL + c) for c in range(0, MEM_W, MXU_DEPTH)]
    acc = x_ref[0]
    log2_a_scale = (-LRU_C * LOG2E) * _softplus_neg(lam_ref[...])
    yc = []
    pre = _lru_gates(xc_ref[0, :, 0:LRU_BW], wg_ref, 0)
    for n in range(LRU_BLOCKS):
        lo, hi = n * LRU_BW, (n + 1) * LRU_BW
        cur = pre
        if n + 1 < LRU_BLOCKS:
            pre = _lru_gates(xc_ref[0, :, hi:hi + LRU_BW], wg_ref, n + 1)
        if independent:
            ref, c, wrow = independent.pop(0)
            acc = acc + _dot(ref[0, :, c:c + MXU_DEPTH], w_ref[wrow:wrow + MXU_DEPTH, :])
        a, b = _lru_coeffs(xc_ref[0, :, lo:hi], cur, ba_ref[:, lo:hi], bi_ref[:, lo:hi], log2_a_scale[:, lo:hi])
        hb, carry_scr[:, lo:hi] = _lru_scan(a, b, carry_scr[:, lo:hi], reverse=True)
        yc.append(((hf_ref[0, :, lo:hi] + hb) * sgc_ref[0, :, lo:hi].astype(F32)).astype(BF16))
        if n % 2 == 1:
            yc_time = _dot(unperm_ref[...], jnp.concatenate(yc, axis=1)).astype(BF16)
            acc = acc + _dot(yc_time, w_ref[lo - LRU_BW:hi, :])
            yc = []
    assert not independent
    out_ref[0] = _rms(acc, fg_ref[...])


def _odd_back(x, xc, hf, sgc, yd, ym, wg, ba, bi, lam, w_out, final_g):
    bsz, t_len, _ = x.shape
    tt = min(TOKEN_TILE, t_len)
    nt = t_len // tt
    act = lambda width: pl.BlockSpec((1, tt, width), lambda b, i: (b, nt - 1 - i, 0))
    row = _const_spec((1, D_MODEL))
    return pl.pallas_call(
        _odd_back_kernel,
        grid=(bsz, nt),
        in_specs=[act(D_MODEL)] * 4 + [act(MEM_W), act(D_MODEL),
                  _const_spec(wg.shape), row, row, row,
                  _const_spec(w_out.shape), row, _const_spec((tt, tt))],
        out_specs=act(D_MODEL),
        out_shape=jax.ShapeDtypeStruct((bsz, t_len, D_MODEL), F32),
        scratch_shapes=[pltpu.VMEM((1, D_MODEL), F32)],
        compiler_params=_COMPILER_PARAMS,
        name="odd_back",
    )(xc, hf, sgc, yd, ym, x, wg, ba.reshape(1, D_MODEL), bi.reshape(1, D_MODEL),
      lam.reshape(1, D_MODEL), w_out, final_g.reshape(1, D_MODEL), _segment_order(tt).T)


def _trunk(x, kv, norm_g, ev_w_in, ev_conv_w, ev_conv_b, ubias, ev_w_out,
           od_w_in, od_conv_w, od_conv_b, od_lru_wa, od_lru_ba, od_lru_wi, od_lru_bi, od_lru_lam,
           od_sgu_g, od_sgu_w, od_sgu_bt, od_w_out, final_g):
    ya, q, k, v, sgb, ym = _even_front(x, norm_g[0], ev_w_in, ev_conv_w, ev_conv_b, kv[0])
    x = _even_back(x, q, k, v, sgb, ya, ym, ubias, ev_w_out)
    wg = jnp.concatenate([od_lru_wa, od_lru_wi], axis=-1)
    xc, hf, sgc, yd, ym = _odd_front(x, norm_g[1], od_w_in, od_conv_w, od_conv_b,
                                     wg[0], od_lru_ba[0], od_lru_bi[0], od_lru_lam[0],
                                     od_sgu_g, od_sgu_w, od_sgu_bt, kv[1])
    return _odd_back(x, xc, hf, sgc, yd, ym, wg[1], od_lru_ba[1], od_lru_bi[1],
                     od_lru_lam[1], od_w_out, final_g)


def kernel(x_prompt, x_sample, mem_prompt, mem_sample, norm_g, mem_norm_g, w_mem_kv, ev_w_in, ev_conv_w,
           ev_conv_b, ev_rpb, ev_w_out, od_w_in, od_conv_w, od_conv_b, od_lru_wa, od_lru_ba, od_lru_wi,
           od_lru_bi, od_lru_lam, od_sgu_g, od_sgu_w, od_sgu_b, od_w_out, final_g):
    depth = norm_g.shape[0]
    assert depth == 2 and ev_w_in.shape[0] == 1 and od_w_in.shape[0] == 1
    nb_p, nb_s = mem_prompt.shape[0], mem_sample.shape[0]
    mem_all = jnp.concatenate([mem_prompt.reshape(-1, D_MODEL), mem_sample.reshape(-1, D_MODEL)], axis=0)
    kv = _memkv(mem_all, mem_norm_g, w_mem_kv.astype(BF16)).reshape(depth, nb_p + nb_s, N_MEM, 2 * MEM_W)
    shared = (norm_g, ev_w_in[0].astype(BF16), ev_conv_w[0], ev_conv_b[0], _na_bias_table(ev_rpb[0]),
              ev_w_out[0].astype(BF16), od_w_in[0].astype(BF16), od_conv_w[0], od_conv_b[0],
              od_lru_wa[0].astype(BF16), od_lru_ba[0], od_lru_wi[0].astype(BF16), od_lru_bi[0], od_lru_lam[0],
              od_sgu_g[0], od_sgu_w[0].astype(BF16), od_sgu_b[0].T, od_w_out[0].astype(BF16), final_g)
    y_prompt = _trunk(x_prompt, kv[:, :nb_p], *shared)
    y_sample = _trunk(x_sample, kv[:, nb_p:], *shared)
    return (y_prompt, y_sample)
```

```python
import functools

import jax
import jax.numpy as jnp
import numpy as np
from jax import lax
from jax.experimental import pallas as pl
from jax.experimental.pallas import tpu as pltpu

F32 = jnp.float32
BF16 = jnp.bfloat16

D_MODEL = 1024
GRID_W = 64
N_MEM = 256
EPS = 1e-6
CONV_K = 3
NA_HEADS = 16
NA_HEAD_DIM = 64
NA_WIN_ROWS = 8
NA_WIN_COLS = 16
LRU_BLOCKS = 8
LRU_BW = 128
LRU_CONV_K = 4
LRU_C = 8.0
SGU_GROUPS = 8
SGU_CHUNK = 128
MEM_HEADS = 4
MEM_HEAD_DIM = 128
MEM_W = MEM_HEADS * MEM_HEAD_DIM

EV_BG, EV_CG, EV_XV, EV_GA, EV_Q, EV_K, EV_V, EV_GB, EV_QM, EV_GM = (
    0, 1024, 2048, 3072, 4096, 5120, 6144, 7168, 8192, 8704)
OD_XR, OD_GC, OD_U, OD_V, OD_GD, OD_QM, OD_GM = (0, 1024, 2048, 3072, 4096, 5120, 5632)

V7X_VMEM_LIMIT_BYTES = 60 * 1024 * 1024
SUBLANES = 8
MXU_DEPTH = 256
HALO = 16
TOKEN_TILE = 512
NA_ROW_BLOCK = 8
NA_PAIRS = NA_HEADS // 2
MASK_BIAS = -1e30
LOG2E = 1.4426950408889634

_COMPILER_PARAMS = pltpu.CompilerParams(
    dimension_semantics=("arbitrary", "arbitrary"), vmem_limit_bytes=V7X_VMEM_LIMIT_BYTES)


def _dot(a, b):
    return jnp.dot(a, b, preferred_element_type=F32)


def _dot_nt(a, b):
    return lax.dot_general(a, b, (((1,), (1,)), ((), ())), preferred_element_type=F32)


def _rms(x, g):
    return x * lax.rsqrt(jnp.mean(x * x, axis=-1, keepdims=True) + EPS) * g


def _sigmoid(x):
    return 1.0 / (1.0 + jnp.exp(-x))


def _silu(x):
    return x * _sigmoid(x)


def _mem_attention_head(qm, gm, kv_ref, out_ref, hh):
    scale = MEM_HEAD_DIM ** -0.5 * LOG2E
    lo, hi = hh * MEM_HEAD_DIM, (hh + 1) * MEM_HEAD_DIM
    s = _dot_nt(qm.astype(BF16), kv_ref[0, :, lo:hi]) * scale
    e = jnp.exp2(s - jnp.max(s, axis=-1, keepdims=True))
    o = _dot(e.astype(BF16), kv_ref[0, :, MEM_W + lo:MEM_W + hi])
    o = o / jnp.sum(e, axis=-1, keepdims=True)
    out_ref[0, :, lo:hi] = (o * _silu(gm)).astype(out_ref.dtype)


def _mem_attention(qm, gm, kv_ref, out_ref, heads):
    for n, hh in enumerate(heads):
        lo, hi = n * MEM_HEAD_DIM, (n + 1) * MEM_HEAD_DIM
        _mem_attention_head(qm[:, lo:hi], gm[:, lo:hi], kv_ref, out_ref, hh)


def _normed_with_halo(x_ref, xp_ref, xn_ref, g):
    h = _rms(x_ref[0], g).astype(BF16)
    hp = _rms(xp_ref[0], g).astype(BF16)
    hn = _rms(xn_ref[0], g).astype(BF16)
    return h, jnp.concatenate([hp, h, hn], axis=0)


def _halo_valid(tt):
    i = pl.program_id(1)
    last = pl.num_programs(1) - 1
    row = lax.broadcasted_iota(jnp.int32, (tt + 2 * HALO, 1), 0)
    return ((row >= HALO) | (i > 0)) & ((row < tt + HALO) | (i < last))


def _memkv_kernel(mem_ref, g_ref, w_ref, kv_ref):
    hn = _rms(mem_ref[...], g_ref[0]).astype(BF16)
    kv_ref[0] = _dot(hn, w_ref[0]).astype(BF16)


def _memkv(mem_all, mem_norm_g, w_mem_kv):
    depth = w_mem_kv.shape[0]
    nb = mem_all.shape[0] // N_MEM
    return pl.pallas_call(
        _memkv_kernel,
        grid=(depth, nb),
        in_specs=[
            pl.BlockSpec((N_MEM, D_MODEL), lambda l, b: (b, 0)),
            pl.BlockSpec((1, 1, D_MODEL), lambda l, b: (l, 0, 0)),
            pl.BlockSpec((1, D_MODEL, 2 * MEM_W), lambda l, b: (l, 0, 0)),
        ],
        out_specs=pl.BlockSpec((1, N_MEM, 2 * MEM_W), lambda l, b: (l, b, 0)),
        out_shape=jax.ShapeDtypeStruct((depth, nb * N_MEM, 2 * MEM_W), BF16),
        compiler_params=_COMPILER_PARAMS,
        name="memkv",
    )(mem_all, mem_norm_g.reshape(depth, 1, D_MODEL), w_mem_kv)


def _even_front_kernel(x_ref, xp_ref, xn_ref, g_ref, w_ref, cw_ref, cb_ref, kv_ref,
                       ya_ref, q_ref, k_ref, v_ref, sgb_ref, ym_ref, p_scr):
    tt = x_ref.shape[1]
    h, h_ext = _normed_with_halo(x_ref, xp_ref, xn_ref, g_ref[...])

    cg = _dot(h_ext, w_ref[:, EV_CG:EV_CG + D_MODEL])
    xv = _dot(h_ext, w_ref[:, EV_XV:EV_XV + D_MODEL])
    p_scr[...] = jnp.where(_halo_valid(tt), cg * xv, 0.0)
    qm = _dot(h, w_ref[:, EV_QM:EV_QM + MEM_W])
    gm = _dot(h, w_ref[:, EV_GM:EV_GM + MEM_W])
    q_ref[0] = (_dot(h, w_ref[:, EV_Q:EV_Q + D_MODEL]) * (NA_HEAD_DIM ** -0.5 * LOG2E)).astype(BF16)
    conv = cb_ref[...]
    for j in range(CONV_K):
        conv = conv + cw_ref[j:j + 1, :] * p_scr[pl.ds(HALO - CONV_K // 2 + j, tt), :]
    k_ref[0] = _dot(h, w_ref[:, EV_K:EV_K + D_MODEL]).astype(BF16)
    _mem_attention(qm[:, :MEM_W // 2], gm[:, :MEM_W // 2], kv_ref, ym_ref, range(MEM_HEADS // 2))
    bg = _dot(h, w_ref[:, EV_BG:EV_BG + D_MODEL])
    ga = _dot(h, w_ref[:, EV_GA:EV_GA + D_MODEL])
    ya_ref[0] = (bg * conv * _silu(ga)).astype(BF16)
    sgb_ref[0] = _silu(_dot(h, w_ref[:, EV_GB:EV_GB + D_MODEL])).astype(BF16)
    _mem_attention(qm[:, MEM_W // 2:], gm[:, MEM_W // 2:], kv_ref, ym_ref, range(MEM_HEADS // 2, MEM_HEADS))
    v_ref[0] = _dot(h, w_ref[:, EV_V:EV_V + D_MODEL]).astype(BF16)


def _tile_specs(tt, t_len):
    per = tt // HALO
    n_halo = t_len // HALO
    return [
        pl.BlockSpec((1, tt, D_MODEL), lambda b, i: (b, i, 0)),
        pl.BlockSpec((1, HALO, D_MODEL), lambda b, i: (b, jnp.maximum(i * per - 1, 0), 0)),
        pl.BlockSpec((1, HALO, D_MODEL), lambda b, i: (b, jnp.minimum((i + 1) * per, n_halo - 1), 0)),
    ]


def _const_spec(shape):
    nd = len(shape)
    return pl.BlockSpec(shape, lambda b, i: (0,) * nd, pipeline_mode=pl.Buffered(1))


def _even_front(x, g, w_in, conv_w, conv_b, kv):
    bsz, t_len, _ = x.shape
    tt = min(TOKEN_TILE, t_len)
    act = lambda width: pl.BlockSpec((1, tt, width), lambda b, i: (b, i, 0))
    out_sd = lambda width: jax.ShapeDtypeStruct((bsz, t_len, width), BF16)
    return pl.pallas_call(
        _even_front_kernel,
        grid=(bsz, t_len // tt),
        in_specs=_tile_specs(tt, t_len) + [
            _const_spec((1, D_MODEL)),
            _const_spec(w_in.shape),
            _const_spec(conv_w.shape),
            _const_spec((1, D_MODEL)),
            pl.BlockSpec((1, N_MEM, 2 * MEM_W), lambda b, i: (b, 0, 0)),
        ],
        out_specs=[act(D_MODEL)] * 5 + [act(MEM_W)],
        out_shape=[out_sd(D_MODEL)] * 5 + [out_sd(MEM_W)],
        scratch_shapes=[pltpu.VMEM((tt + 2 * HALO, D_MODEL), F32)],
        compiler_params=_COMPILER_PARAMS,
        name="even_front",
    )(x, x, x, g.reshape(1, D_MODEL), w_in, conv_w, conv_b.reshape(1, D_MODEL), kv)


def _na_bias_table(rpb):
    col = np.arange(GRID_W)
    cstart = np.clip(col - NA_WIN_COLS // 2, 0, GRID_W - NA_WIN_COLS)
    col_ok = (col[None, :] >= cstart[:, None]) & (col[None, :] < cstart[:, None] + NA_WIN_COLS)
    dc_idx = np.clip(col[None, :] - col[:, None], -(NA_WIN_COLS - 1), NA_WIN_COLS - 1) + NA_WIN_COLS - 1
    nrel = 2 * NA_WIN_COLS - 1
    select = ((dc_idx[None] == np.arange(nrel)[:, None, None]) & col_ok[None]).astype(np.float32)
    select2 = np.zeros((2, nrel, GRID_W, 2, GRID_W), np.float32)
    select2[0, :, :, 0, :] = select
    select2[1, :, :, 1, :] = select
    select2 = select2.reshape(2 * nrel, GRID_W, 2 * GRID_W)
    mask = np.tile(np.where(col_ok, 0.0, MASK_BIAS).astype(np.float32), (1, 2))
    scaled = rpb.astype(F32) * LOG2E
    rows2 = jnp.concatenate([scaled[:, :-1], scaled[:, 1:]], axis=-1)
    two = jnp.einsum("hdm,mcl->dhcl", rows2, select2, precision=lax.Precision.HIGHEST) + mask
    return two.reshape(2 * NA_WIN_ROWS - 2, NA_PAIRS, 2 * GRID_W, 2 * GRID_W)


def _even_back_kernel(q_ref, k0_ref, kn_ref, v0_ref, vn_ref, sgb_ref, ya_ref, ym_ref,
                      x_ref, u_ref, w_ref, out_ref, kwin, vwin, yb_scr, s_scr, m_scr):
    rbt = q_ref.shape[1]
    rb = rbt // GRID_W
    j = pl.program_id(1)
    rows = pl.num_programs(1) * rb
    win_keys = NA_WIN_ROWS * GRID_W

    @pl.when(j == 0)
    def _():
        kwin[rbt:2 * rbt] = k0_ref[0]
        vwin[rbt:2 * rbt] = v0_ref[0]

    @pl.when(j > 0)
    def _():
        kwin[0:2 * rbt] = kwin[rbt:3 * rbt]
        vwin[0:2 * rbt] = vwin[rbt:3 * rbt]

    kwin[2 * rbt:3 * rbt] = kn_ref[0]
    vwin[2 * rbt:3 * rbt] = vn_ref[0]

    lane = lax.broadcasted_iota(jnp.int32, (GRID_W, 2 * NA_HEAD_DIM), 1)
    first_head = lane < NA_HEAD_DIM

    independent = [(ya_ref, c, c) for c in range(0, D_MODEL, MXU_DEPTH)]
    independent += [(ym_ref, c, 2 * D_MODEL + c) for c in range(0, MEM_W, MXU_DEPTH)]
    out_ref[0] = x_ref[0]

    for i in range(rb):
        r = j * rb + i
        rstart = jnp.clip(r - NA_WIN_ROWS // 2, 0, rows - NA_WIN_ROWS)
        local = pl.multiple_of((rstart - (j - 1) * rb) * GRID_W, GRID_W)
        off = rstart - r + NA_WIN_ROWS - 1
        qrows = slice(i * GRID_W, (i + 1) * GRID_W)
        for p in range(NA_PAIRS):
            lo, hi = p * 2 * NA_HEAD_DIM, (p + 1) * 2 * NA_HEAD_DIM
            qp = q_ref[0, qrows, lo:hi]
            zero = jnp.zeros_like(qp)
            q2 = jnp.concatenate([jnp.where(first_head, qp, zero), jnp.where(first_head, zero, qp)], axis=0)
            s = _dot_nt(q2, kwin[pl.ds(local, win_keys), lo:hi])
            bias = jnp.concatenate([u_ref[off + 2 * m, p] for m in range(NA_WIN_ROWS // 2)], axis=1)
            s = s + bias
            s_scr[p] = s
            m_scr[p] = jnp.broadcast_to(jnp.max(s, axis=-1, keepdims=True), (2 * GRID_W, 2 * NA_HEAD_DIM))
        if independent:
            ref, c, wrow = independent.pop(0)
            out_ref[0] += _dot(ref[0, :, c:c + MXU_DEPTH], w_ref[wrow:wrow + MXU_DEPTH, :])
        for p in range(NA_PAIRS):
            lo, hi = p * 2 * NA_HEAD_DIM, (p + 1) * 2 * NA_HEAD_DIM
            m = m_scr[p]
            e = jnp.concatenate(
                [jnp.exp2(s_scr[p, :, c * 128:(c + 1) * 128] - m) for c in range(win_keys // 128)], axis=1)
            o2 = _dot(e.astype(BF16), vwin[pl.ds(local, win_keys), lo:hi])
            o2 = o2 / jnp.sum(e, axis=-1, keepdims=True)
            o = jnp.where(first_head, o2[:GRID_W], o2[GRID_W:])
            gate = sgb_ref[0, qrows, lo:hi].astype(F32)
            yb_scr[qrows, lo:hi] = (o * gate).astype(BF16)
    assert not independent

    out_ref[0] += _dot(yb_scr[...], w_ref[D_MODEL:2 * D_MODEL, :])


def _even_back(x, q, k, v, sgb, ya, ym, ubias, w_out):
    bsz, t_len, _ = x.shape
    rbt = NA_ROW_BLOCK * GRID_W
    nblk = t_len // rbt
    cur = lambda width: pl.BlockSpec((1, rbt, width), lambda b, j: (b, j, 0))
    first = pl.BlockSpec((1, rbt, D_MODEL), lambda b, j: (b, 0, 0))
    nxt = pl.BlockSpec((1, rbt, D_MODEL), lambda b, j: (b, jnp.minimum(j + 1, nblk - 1), 0))
    return pl.pallas_call(
        _even_back_kernel,
        grid=(bsz, nblk),
        in_specs=[cur(D_MODEL), first, nxt, first, nxt,
                  cur(D_MODEL), cur(D_MODEL), cur(MEM_W), cur(D_MODEL),
                  _const_spec(ubias.shape), _const_spec(w_out.shape)],
        out_specs=cur(D_MODEL),
        out_shape=jax.ShapeDtypeStruct((bsz, t_len, D_MODEL), F32),
        scratch_shapes=[pltpu.VMEM((3 * rbt, D_MODEL), BF16), pltpu.VMEM((3 * rbt, D_MODEL), BF16),
                        pltpu.VMEM((rbt, D_MODEL), BF16),
                        pltpu.VMEM((NA_PAIRS, 2 * GRID_W, NA_WIN_ROWS * GRID_W), F32),
                        pltpu.VMEM((NA_PAIRS, 2 * GRID_W, 2 * NA_HEAD_DIM), F32)],
        compiler_params=_COMPILER_PARAMS,
        name="even_back",
    )(q, k, k, v, v, sgb, ya, ym, x, ubias, w_out)


def _lru_gates(xc, wg_ref, n):
    return _dot(xc.astype(BF16), wg_ref[n])


def _lru_coeffs(xc, pre, ba, bi, log2_a_scale):
    r = _sigmoid(pre[:, :LRU_BW] + ba)
    gate_i = _sigmoid(pre[:, LRU_BW:] + bi)
    a = jnp.exp2(r * log2_a_scale)
    return a, jnp.sqrt(1.0 - a * a) * (gate_i * xc)


def _softplus_neg(lam):
    return jnp.maximum(-lam, 0.0) + jnp.log(1.0 + jnp.exp(-jnp.abs(lam)))


def _segment_order(tt):
    p = jnp.arange(tt)
    t = (p % SUBLANES) * (tt // SUBLANES) + p // SUBLANES
    return (t[:, None] == jnp.arange(tt)[None, :]).astype(BF16)


def _scan_sublanes(a, b, reverse):
    pos = lax.broadcasted_iota(jnp.int32, a.shape, 0)
    k = 1
    while k < SUBLANES:
        ok = (pos < SUBLANES - k) if reverse else (pos >= k)
        shift = SUBLANES - k if reverse else k
        a_s = jnp.where(ok, pltpu.roll(a, shift, 0), 1.0)
        b_s = jnp.where(ok, pltpu.roll(b, shift, 0), 0.0)
        b = a * b_s + b
        a = a * a_s
        k *= 2
    return a, b


def _lru_scan(a, b, carry, reverse):
    tt, width = a.shape
    seg = tt // SUBLANES
    a = a.reshape(seg, SUBLANES, width)
    b = b.reshape(seg, SUBLANES, width)
    steps = range(seg - 1, -1, -1) if reverse else range(seg)
    a_tot, h_tot = None, None
    for j in steps:
        if a_tot is None:
            a_tot, h_tot = a[j], b[j]
        else:
            a_tot, h_tot = a_tot * a[j], a[j] * h_tot + b[j]
    a_cum, h_cum = _scan_sublanes(a_tot, h_tot, reverse)
    leaving = a_cum * carry + h_cum
    pos = lax.broadcasted_iota(jnp.int32, leaving.shape, 0)
    if reverse:
        h = jnp.where(pos == SUBLANES - 1, carry, pltpu.roll(leaving, SUBLANES - 1, 0))
        carry = leaving[0:1, :]
    else:
        h = jnp.where(pos == 0, carry, pltpu.roll(leaving, 1, 0))
        carry = leaving[SUBLANES - 1:SUBLANES, :]
    hs = [None] * seg
    for j in steps:
        h = a[j] * h + b[j]
        hs[j] = h
    return jnp.concatenate(hs, axis=0), carry


def _odd_front_kernel(x_ref, xp_ref, xn_ref, g_ref, w_ref, cw_ref, cb_ref, wg_ref, ba_ref, bi_ref,
                      lam_ref, sg_ref, sw_ref, sbt_ref, kv_ref, perm_ref,
                      xc_ref, hf_ref, sgc_ref, yd_ref, ym_ref, xr_scr, carry_scr, z_scr, vn_scr):
    tt = x_ref.shape[1]
    seg = tt // SUBLANES
    i = pl.program_id(1)
    h, h_ext = _normed_with_halo(x_ref, xp_ref, xn_ref, g_ref[...])
    hs = _dot(perm_ref[...], h).astype(BF16)

    @pl.when(i == 0)
    def _():
        carry_scr[...] = jnp.zeros_like(carry_scr)

    lhs = jnp.concatenate([hs, h_ext[:HALO], h_ext[HALO + tt:]], axis=0)
    xr = _dot(lhs, w_ref[:, OD_XR:OD_XR + D_MODEL])

    def project(col):
        if col < OD_U:
            z = _dot(hs, w_ref[:, col:col + MXU_DEPTH])
            sgc_ref[0, :, col - OD_GC:col - OD_GC + MXU_DEPTH] = _silu(z).astype(BF16)
        else:
            z_scr[:, col - OD_U:col - OD_U + MXU_DEPTH] = _dot(h, w_ref[:, col:col + MXU_DEPTH])

    def mem_head(hh):
        qcol = OD_QM - OD_U + hh * MEM_HEAD_DIM
        gcol = OD_GM - OD_U + hh * MEM_HEAD_DIM
        _mem_attention_head(z_scr[:, qcol:qcol + MEM_HEAD_DIM], z_scr[:, gcol:gcol + MEM_HEAD_DIM],
                            kv_ref, ym_ref, hh)

    def norm_v():
        vn_scr[...] = _rms(z_scr[:, OD_V - OD_U:OD_GD - OD_U], sg_ref[...]).astype(BF16)

    def sgu_group(grp):
        lo, hi = grp * LRU_BW, (grp + 1) * LRU_BW
        chunks = range(tt // SGU_CHUNK)
        vn = jnp.concatenate([vn_scr[c * SGU_CHUNK:(c + 1) * SGU_CHUNK, lo:hi] for c in chunks], axis=1)
        mixed_all = _dot(sw_ref[grp], vn)
        for c in chunks:
            r0, r1 = c * SGU_CHUNK, (c + 1) * SGU_CHUNK
            mixed = mixed_all[:, c * LRU_BW:(c + 1) * LRU_BW] + sbt_ref[:, grp:grp + 1]
            u = z_scr[r0:r1, lo:hi]
            sgd = _silu(z_scr[r0:r1, OD_GD - OD_U + lo:OD_GD - OD_U + hi])
            yd_ref[0, r0:r1, lo:hi] = (u * mixed * sgd).astype(BF16)

    slices = lambda start: [functools.partial(project, start + c) for c in range(0, D_MODEL, MXU_DEPTH)]
    for col in range(OD_QM, w_ref.shape[1], MXU_DEPTH):
        project(col)
    uvg = slices(OD_V) + slices(OD_U) + slices(OD_GD)
    gating = lambda a: [functools.partial(sgu_group, g) for g in (a, a + 1)]
    side = [
        (uvg[0:3], [functools.partial(mem_head, 0)]),
        (uvg[3:6], [functools.partial(mem_head, 1), norm_v]),
        (uvg[6:9], [functools.partial(mem_head, 2)]),
        (uvg[9:12], [functools.partial(mem_head, 3)]),
        (slices(OD_GC)[0:1], gating(0)),
        (slices(OD_GC)[1:2], gating(2)),
        (slices(OD_GC)[2:3], gating(4)),
        (slices(OD_GC)[3:4], gating(6)),
    ]
    assert len(side) == LRU_BLOCKS and MEM_HEADS == 4

    before =jnp.where(i > 0, xr[tt + HALO - 2:tt + HALO], 0.0)
    after = jnp.where(i < pl.num_programs(1) - 1, xr[tt + HALO:tt + HALO + 1], 0.0)
    sub = lax.broadcasted_iota(jnp.int32, (SUBLANES, 1), 0)
    lead = LRU_CONV_K // 2
    for d in range(lead):
        slab = pltpu.roll(xr[(seg + d - lead) * SUBLANES:(seg + d - lead + 1) * SUBLANES], 1, 0)
        xr_scr[d * SUBLANES:(d + 1) * SUBLANES] = jnp.where(sub == 0, before[d:d + 1], slab)
    xr_scr[lead * SUBLANES:lead * SUBLANES + tt] = xr[:tt]
    slab = pltpu.roll(xr[0:SUBLANES], SUBLANES - 1, 0)
    xr_scr[lead * SUBLANES + tt:] = jnp.where(sub == SUBLANES - 1, after, slab)
    xc = cb_ref[...]
    for j in range(LRU_CONV_K):
        xc = xc + cw_ref[j:j + 1, :] * xr_scr[j * SUBLANES:j * SUBLANES + tt, :]
    xc_ref[0] = xc

    log2_a_scale = (-LRU_C * LOG2E) * _softplus_neg(lam_ref[...])
    pre = _lru_gates(xc[:, 0:LRU_BW], wg_ref, 0)
    for n in range(LRU_BLOCKS):
        lo, hi = n * LRU_BW, (n + 1) * LRU_BW
        cur = pre
        if n + 1 < LRU_BLOCKS:
            pre = _lru_gates(xc[:, hi:hi + LRU_BW], wg_ref, n + 1)
        projections, consumers = side[n]
        for piece in projections:
            piece()
        for piece in consumers:
            piece()
        a, b = _lru_coeffs(xc[:, lo:hi], cur, ba_ref[:, lo:hi], bi_ref[:, lo:hi], log2_a_scale[:, lo:hi])
        hf_ref[0, :, lo:hi], carry_scr[:, lo:hi] = _lru_scan(a, b, carry_scr[:, lo:hi], reverse=False)


def _odd_front(x, g, w_in, conv_w, conv_b, wg, ba, bi, lam, sgu_g, sgu_w, sgu_bt, kv):
    bsz, t_len, _ = x.shape
    tt = min(TOKEN_TILE, t_len)
    act = lambda width: pl.BlockSpec((1, tt, width), lambda b, i: (b, i, 0))
    sd = lambda width, dt: jax.ShapeDtypeStruct((bsz, t_len, width), dt)
    row = _const_spec((1, D_MODEL))
    return pl.pallas_call(
        _odd_front_kernel,
        grid=(bsz, t_len // tt),
        in_specs=_tile_specs(tt, t_len) + [
            row, _const_spec(w_in.shape), _const_spec(conv_w.shape), row,
            _const_spec(wg.shape), row, row, row,
            row, _const_spec(sgu_w.shape), _const_spec(sgu_bt.shape),
            pl.BlockSpec((1, N_MEM, 2 * MEM_W), lambda b, i: (b, 0, 0)),
            _const_spec((tt, tt)),
        ],
        out_specs=[act(D_MODEL)] * 4 + [act(MEM_W)],
        out_shape=[sd(D_MODEL, F32), sd(D_MODEL, F32), sd(D_MODEL, BF16), sd(D_MODEL, BF16), sd(MEM_W, BF16)],
        scratch_shapes=[pltpu.VMEM((tt + (LRU_CONV_K - 1) * SUBLANES, D_MODEL), F32),
                        pltpu.VMEM((1, D_MODEL), F32),
                        pltpu.VMEM((tt, w_in.shape[1] - OD_U), F32),
                        pltpu.VMEM((tt, D_MODEL), BF16)],
        compiler_params=_COMPILER_PARAMS,
        name="odd_front",
    )(x, x, x, g.reshape(1, D_MODEL), w_in, conv_w, conv_b.reshape(1, D_MODEL), wg,
      ba.reshape(1, D_MODEL), bi.reshape(1, D_MODEL), lam.reshape(1, D_MODEL),
      sgu_g.reshape(1, D_MODEL), sgu_w, sgu_bt, kv, _segment_order(tt))


def _odd_back_kernel(xc_ref, hf_ref, sgc_ref, yd_ref, ym_ref, x_ref, wg_ref, ba_ref, bi_ref, lam_ref,
                     w_ref, fg_ref, unperm_ref, out_ref, carry_scr):
    @pl.when(pl.program_id(1) == 0)
    def _():
        carry_scr[...] = jnp.zeros_like(carry_scr)

    independent = [(yd_ref, c, D_MODEL + c) for c in range(0, D_MODEL, MXU_DEPTH)]
    independent += [(ym_ref, c, 2 * D_MODEL + c) for c in range(0, MEM_W, MXU_DEPTH)]
    acc = x_ref[0]
    log2_a_scale = (-LRU_C * LOG2E) * _softplus_neg(lam_ref[...])
    yc = []
    pre = _lru_gates(xc_ref[0, :, 0:LRU_BW], wg_ref, 0)
    for n in range(LRU_BLOCKS):
        lo, hi = n * LRU_BW, (n + 1) * LRU_BW
        cur = pre
        if n + 1 < LRU_BLOCKS:
            pre = _lru_gates(xc_ref[0, :, hi:hi + LRU_BW], wg_ref, n + 1)
        if independent:
            ref, c, wrow = independent.pop(0)
            acc = acc + _dot(ref[0, :, c:c + MXU_DEPTH], w_ref[wrow:wrow + MXU_DEPTH, :])
        a, b = _lru_coeffs(xc_ref[0, :, lo:hi], cur, ba_ref[:, lo:hi], bi_ref[:, lo:hi], log2_a_scale[:, lo:hi])
        hb, carry_scr[:, lo:hi] = _lru_scan(a, b, carry_scr[:, lo:hi], reverse=True)
        yc.append(((hf_ref[0, :, lo:hi] + hb) * sgc_ref[0, :, lo:hi].astype(F32)).astype(BF16))
        if n % 2 == 1:
            yc_time = _dot(unperm_ref[...], jnp.concatenate(yc, axis=1)).astype(BF16)
            acc = acc + _dot(yc_time, w_ref[lo - LRU_BW:hi, :])
            yc = []
    assert not independent
    out_ref[0] = _rms(acc, fg_ref[...])


def _odd_back(x, xc, hf, sgc, yd, ym, wg, ba, bi, lam, w_out, final_g):
    bsz, t_len, _ = x.shape
    tt = min(TOKEN_TILE, t_len)
    nt = t_len // tt
    act = lambda width: pl.BlockSpec((1, tt, width), lambda b, i: (b, nt - 1 - i, 0))
    row = _const_spec((1, D_MODEL))
    return pl.pallas_call(
        _odd_back_kernel,
        grid=(bsz, nt),
        in_specs=[act(D_MODEL)] * 4 + [act(MEM_W), act(D_MODEL),
                  _const_spec(wg.shape), row, row, row,
                  _const_spec(w_out.shape), row, _const_spec((tt, tt))],
        out_specs=act(D_MODEL),
        out_shape=jax.ShapeDtypeStruct((bsz, t_len, D_MODEL), F32),
        scratch_shapes=[pltpu.VMEM((1, D_MODEL), F32)],
        compiler_params=_COMPILER_PARAMS,
        name="odd_back",
    )(xc, hf, sgc, yd, ym, x, wg, ba.reshape(1, D_MODEL), bi.reshape(1, D_MODEL),
      lam.reshape(1, D_MODEL), w_out, final_g.reshape(1, D_MODEL), _segment_order(tt).T)


def _trunk(x, kv, norm_g, ev_w_in, ev_conv_w, ev_conv_b, ubias, ev_w_out,
           od_w_in, od_conv_w, od_conv_b, od_lru_wa, od_lru_ba, od_lru_wi, od_lru_bi, od_lru_lam,
           od_sgu_g, od_sgu_w, od_sgu_bt, od_w_out, final_g):
    ya, q, k, v, sgb, ym = _even_front(x, norm_g[0], ev_w_in, ev_conv_w, ev_conv_b, kv[0])
    x = _even_back(x, q, k, v, sgb, ya, ym, ubias, ev_w_out)
    wg = jnp.concatenate([od_lru_wa, od_lru_wi], axis=-1)
    xc, hf, sgc, yd, ym = _odd_front(x, norm_g[1], od_w_in, od_conv_w, od_conv_b,
                                     wg[0], od_lru_ba[0], od_lru_bi[0], od_lru_lam[0],
                                     od_sgu_g, od_sgu_w, od_sgu_bt, kv[1])
    return _odd_back(x, xc, hf, sgc, yd, ym, wg[1], od_lru_ba[1], od_lru_bi[1],
                     od_lru_lam[1], od_w_out, final_g)


def kernel(x_prompt, x_sample, mem_prompt, mem_sample, norm_g, mem_norm_g, w_mem_kv, ev_w_in, ev_conv_w,
           ev_conv_b, ev_rpb, ev_w_out, od_w_in, od_conv_w, od_conv_b, od_lru_wa, od_lru_ba, od_lru_wi,
           od_lru_bi, od_lru_lam, od_sgu_g, od_sgu_w, od_sgu_b, od_w_out, final_g):
    depth = norm_g.shape[0]
    assert depth == 2 and ev_w_in.shape[0] == 1 and od_w_in.shape[0] == 1
    nb_p, nb_s = mem_prompt.shape[0], mem_sample.shape[0]
    mem_all = jnp.concatenate([mem_prompt.reshape(-1, D_MODEL), mem_sample.reshape(-1, D_MODEL)], axis=0)
    kv = _memkv(mem_all, mem_norm_g, w_mem_kv.astype(BF16)).reshape(depth, nb_p + nb_s, N_MEM, 2 * MEM_W)
    shared = (norm_g, ev_w_in[0].astype(BF16), ev_conv_w[0], ev_conv_b[0], _na_bias_table(ev_rpb[0]),
              ev_w_out[0].astype(BF16), od_w_in[0].astype(BF16), od_conv_w[0], od_conv_b[0],
              od_lru_wa[0].astype(BF16), od_lru_ba[0], od_lru_wi[0].astype(BF16), od_lru_bi[0], od_lru_lam[0],
              od_sgu_g[0], od_sgu_w[0].astype(BF16), od_sgu_b[0].T, od_w_out[0].astype(BF16), final_g)
    y_prompt = _trunk(x_prompt, kv[:, :nb_p], *shared)
    y_sample = _trunk(x_sample, kv[:, nb_p:], *shared)
    return (y_prompt, y_sample)
```

```python
import functools

import jax
import jax.numpy as jnp
import numpy as np
from jax import lax
from jax.experimental import pallas as pl
from jax.experimental.pallas import tpu as pltpu

F32 = jnp.float32
BF16 = jnp.bfloat16

D_MODEL = 1024
GRID_W = 64
N_MEM = 256
EPS = 1e-6
CONV_K = 3
NA_HEADS = 16
NA_HEAD_DIM = 64
NA_WIN_ROWS = 8
NA_WIN_COLS = 16
LRU_BLOCKS = 8
LRU_BW = 128
LRU_CONV_K = 4
LRU_C = 8.0
SGU_GROUPS = 8
SGU_CHUNK = 128
MEM_HEADS = 4
MEM_HEAD_DIM = 128
MEM_W = MEM_HEADS * MEM_HEAD_DIM

EV_BG, EV_CG, EV_XV, EV_GA, EV_Q, EV_K, EV_V, EV_GB, EV_QM, EV_GM = (
    0, 1024, 2048, 3072, 4096, 5120, 6144, 7168, 8192, 8704)
OD_XR, OD_GC, OD_U, OD_V, OD_GD, OD_QM, OD_GM = (0, 1024, 2048, 3072, 4096, 5120, 5632)

V7X_VMEM_LIMIT_BYTES = 60 * 1024 * 1024
SUBLANES = 8
MXU_DEPTH = 256
HALO = 16
TOKEN_TILE = 512
NA_ROW_BLOCK = 8
NA_PAIRS = NA_HEADS // 2
MASK_BIAS = -1e30
LOG2E = 1.4426950408889634

_COMPILER_PARAMS = pltpu.CompilerParams(
    dimension_semantics=("arbitrary", "arbitrary"), vmem_limit_bytes=V7X_VMEM_LIMIT_BYTES)


def _dot(a, b):
    return jnp.dot(a, b, preferred_element_type=F32)


def _dot_nt(a, b):
    return lax.dot_general(a, b, (((1,), (1,)), ((), ())), preferred_element_type=F32)


def _rms(x, g):
    return x * lax.rsqrt(jnp.mean(x * x, axis=-1, keepdims=True) + EPS) * g


def _sigmoid(x):
    return 1.0 / (1.0 + jnp.exp(-x))


def _silu(x):
    return x * _sigmoid(x)


def _mem_attention_head(qm, gm, kv_ref, out_ref, hh):
    scale = MEM_HEAD_DIM ** -0.5 * LOG2E
    lo, hi = hh * MEM_HEAD_DIM, (hh + 1) * MEM_HEAD_DIM
    s = _dot_nt(qm.astype(BF16), kv_ref[0, :, lo:hi]) * scale
    e = jnp.exp2(s - jnp.max(s, axis=-1, keepdims=True))
    o = _dot(e.astype(BF16), kv_ref[0, :, MEM_W + lo:MEM_W + hi])
    o = o / jnp.sum(e, axis=-1, keepdims=True)
    out_ref[0, :, lo:hi] = (o * _silu(gm)).astype(out_ref.dtype)


def _mem_attention(qm, gm, kv_ref, out_ref, heads):
    for n, hh in enumerate(heads):
        lo, hi = n * MEM_HEAD_DIM, (n + 1) * MEM_HEAD_DIM
        _mem_attention_head(qm[:, lo:hi], gm[:, lo:hi], kv_ref, out_ref, hh)


def _normed_with_halo(x_ref, xp_ref, xn_ref, g):
    h = _rms(x_ref[0], g).astype(BF16)
    hp = _rms(xp_ref[0], g).astype(BF16)
    hn = _rms(xn_ref[0], g).astype(BF16)
    return h, jnp.concatenate([hp, h, hn], axis=0)


def _halo_valid(tt):
    i = pl.program_id(1)
    last = pl.num_programs(1) - 1
    row = lax.broadcasted_iota(jnp.int32, (tt + 2 * HALO, 1), 0)
    return ((row >= HALO) | (i > 0)) & ((row < tt + HALO) | (i < last))


def _memkv_kernel(mem_ref, g_ref, w_ref, kv_ref):
    hn = _rms(mem_ref[...], g_ref[0]).astype(BF16)
    kv_ref[0] = _dot(hn, w_ref[0]).astype(BF16)


def _memkv(mem_all, mem_norm_g, w_mem_kv):
    depth = w_mem_kv.shape[0]
    rows = mem_all.shape[0]
    return pl.pallas_call(
        _memkv_kernel,
        grid=(depth, 1),
        in_specs=[
            pl.BlockSpec((rows, D_MODEL), lambda l, b: (0, 0)),
            pl.BlockSpec((1, 1, D_MODEL), lambda l, b: (l, 0, 0)),
            pl.BlockSpec((1, D_MODEL, 2 * MEM_W), lambda l, b: (l, 0, 0)),
        ],
        out_specs=pl.BlockSpec((1, rows, 2 * MEM_W), lambda l, b: (l, 0, 0)),
        out_shape=jax.ShapeDtypeStruct((depth, rows, 2 * MEM_W), BF16),
        compiler_params=_COMPILER_PARAMS,
        name="memkv",
    )(mem_all, mem_norm_g.reshape(depth, 1, D_MODEL), w_mem_kv)


def _even_front_kernel(x_ref, xp_ref, xn_ref, g_ref, w_ref, cw_ref, cb_ref, kv_ref,
                       ya_ref, q_ref, k_ref, v_ref, sgb_ref, ym_ref, p_scr):
    tt = x_ref.shape[1]
    h, h_ext = _normed_with_halo(x_ref, xp_ref, xn_ref, g_ref[...])

    cg = _dot(h_ext, w_ref[:, EV_CG:EV_CG + D_MODEL])
    xv = _dot(h_ext, w_ref[:, EV_XV:EV_XV + D_MODEL])
    p_scr[...] = jnp.where(_halo_valid(tt), cg * xv, 0.0)
    qm = _dot(h, w_ref[:, EV_QM:EV_QM + MEM_W])
    gm = _dot(h, w_ref[:, EV_GM:EV_GM + MEM_W])
    q_ref[0] = (_dot(h, w_ref[:, EV_Q:EV_Q + D_MODEL]) * (NA_HEAD_DIM ** -0.5 * LOG2E)).astype(BF16)
    conv = cb_ref[...]
    for j in range(CONV_K):
        conv = conv + cw_ref[j:j + 1, :] * p_scr[pl.ds(HALO - CONV_K // 2 + j, tt), :]
    k_ref[0] = _dot(h, w_ref[:, EV_K:EV_K + D_MODEL]).astype(BF16)
    _mem_attention(qm[:, :MEM_W // 2], gm[:, :MEM_W // 2], kv_ref, ym_ref, range(MEM_HEADS // 2))
    bg = _dot(h, w_ref[:, EV_BG:EV_BG + D_MODEL])
    ga = _dot(h, w_ref[:, EV_GA:EV_GA + D_MODEL])
    ya_ref[0] = (bg * conv * _silu(ga)).astype(BF16)
    sgb_ref[0] = _silu(_dot(h, w_ref[:, EV_GB:EV_GB + D_MODEL])).astype(BF16)
    _mem_attention(qm[:, MEM_W // 2:], gm[:, MEM_W // 2:], kv_ref, ym_ref, range(MEM_HEADS // 2, MEM_HEADS))
    v_ref[0] = _dot(h, w_ref[:, EV_V:EV_V + D_MODEL]).astype(BF16)


def _tile_specs(tt, t_len):
    per = tt // HALO
    n_halo = t_len // HALO
    return [
        pl.BlockSpec((1, tt, D_MODEL), lambda b, i: (b, i, 0)),
        pl.BlockSpec((1, HALO, D_MODEL), lambda b, i: (b, jnp.maximum(i * per - 1, 0), 0)),
        pl.BlockSpec((1, HALO, D_MODEL), lambda b, i: (b, jnp.minimum((i + 1) * per, n_halo - 1), 0)),
    ]


def _const_spec(shape):
    nd = len(shape)
    return pl.BlockSpec(shape, lambda b, i: (0,) * nd, pipeline_mode=pl.Buffered(1))


def _even_front(x, g, w_in, conv_w, conv_b, kv):
    bsz, t_len, _ = x.shape
    tt = min(TOKEN_TILE, t_len)
    act = lambda width: pl.BlockSpec((1, tt, width), lambda b, i: (b, i, 0))
    out_sd = lambda width: jax.ShapeDtypeStruct((bsz, t_len, width), BF16)
    return pl.pallas_call(
        _even_front_kernel,
        grid=(bsz, t_len // tt),
        in_specs=_tile_specs(tt, t_len) + [
            _const_spec((1, D_MODEL)),
            _const_spec(w_in.shape),
            _const_spec(conv_w.shape),
            _const_spec((1, D_MODEL)),
            pl.BlockSpec((1, N_MEM, 2 * MEM_W), lambda b, i: (b, 0, 0)),
        ],
        out_specs=[act(D_MODEL)] * 5 + [act(MEM_W)],
        out_shape=[out_sd(D_MODEL)] * 5 + [out_sd(MEM_W)],
        scratch_shapes=[pltpu.VMEM((tt + 2 * HALO, D_MODEL), F32)],
        compiler_params=_COMPILER_PARAMS,
        name="even_front",
    )(x, x, x, g.reshape(1, D_MODEL), w_in, conv_w, conv_b.reshape(1, D_MODEL), kv)


def _na_bias_table(rpb):
    col = np.arange(GRID_W)
    cstart = np.clip(col - NA_WIN_COLS // 2, 0, GRID_W - NA_WIN_COLS)
    col_ok = (col[None, :] >= cstart[:, None]) & (col[None, :] < cstart[:, None] + NA_WIN_COLS)
    dc_idx = np.clip(col[None, :] - col[:, None], -(NA_WIN_COLS - 1), NA_WIN_COLS - 1) + NA_WIN_COLS - 1
    nrel = 2 * NA_WIN_COLS - 1
    select = ((dc_idx[None] == np.arange(nrel)[:, None, None]) & col_ok[None]).astype(np.float32)
    select2 = np.zeros((2, nrel, GRID_W, 2, GRID_W), np.float32)
    select2[0, :, :, 0, :] = select
    select2[1, :, :, 1, :] = select
    select2 = select2.reshape(2 * nrel, GRID_W, 2 * GRID_W)
    mask = np.tile(np.where(col_ok, 0.0, MASK_BIAS).astype(np.float32), (1, 2))
    scaled = rpb.astype(F32) * LOG2E
    rows2 = jnp.concatenate([scaled[:, :-1], scaled[:, 1:]], axis=-1)
    two = jnp.einsum("hdm,mcl->dhcl", rows2, select2, precision=lax.Precision.HIGHEST) + mask
    return two.reshape(2 * NA_WIN_ROWS - 2, NA_PAIRS, 2 * GRID_W, 2 * GRID_W)


def _even_back_kernel(q_ref, k0_ref, kn_ref, v0_ref, vn_ref, sgb_ref, ya_ref, ym_ref,
                      x_ref, u_ref, w_ref, out_ref, kwin, vwin, yb_scr, s_scr, m_scr):
    rbt = q_ref.shape[1]
    rb = rbt // GRID_W
    j = pl.program_id(1)
    rows = pl.num_programs(1) * rb
    win_keys = NA_WIN_ROWS * GRID_W

    @pl.when(j == 0)
    def _():
        kwin[rbt:2 * rbt] = k0_ref[0]
        vwin[rbt:2 * rbt] = v0_ref[0]

    @pl.when(j > 0)
    def _():
        kwin[0:2 * rbt] = kwin[rbt:3 * rbt]
        vwin[0:2 * rbt] = vwin[rbt:3 * rbt]

    kwin[2 * rbt:3 * rbt] = kn_ref[0]
    vwin[2 * rbt:3 * rbt] = vn_ref[0]

    lane = lax.broadcasted_iota(jnp.int32, (GRID_W, 2 * NA_HEAD_DIM), 1)
    first_head = lane < NA_HEAD_DIM

    independent = [(ya_ref, c, c) for c in range(0, D_MODEL, MXU_DEPTH)]
    independent += [(ym_ref, c, 2 * D_MODEL + c) for c in range(0, MEM_W, MXU_DEPTH)]
    out_ref[0] = x_ref[0]

    for i in range(rb):
        r = j * rb + i
        rstart = jnp.clip(r - NA_WIN_ROWS // 2, 0, rows - NA_WIN_ROWS)
        local = pl.multiple_of((rstart - (j - 1) * rb) * GRID_W, GRID_W)
        off = rstart - r + NA_WIN_ROWS - 1
        qrows = slice(i * GRID_W, (i + 1) * GRID_W)
        for p in range(NA_PAIRS):
            lo, hi = p * 2 * NA_HEAD_DIM, (p + 1) * 2 * NA_HEAD_DIM
            qp = q_ref[0, qrows, lo:hi]
            zero = jnp.zeros_like(qp)
            q2 = jnp.concatenate([jnp.where(first_head, qp, zero), jnp.where(first_head, zero, qp)], axis=0)
            s = _dot_nt(q2, kwin[pl.ds(local, win_keys), lo:hi])
            bias = jnp.concatenate([u_ref[off + 2 * m, p] for m in range(NA_WIN_ROWS // 2)], axis=1)
            s = s + bias
            s_scr[p] = s
            m_scr[p] = jnp.broadcast_to(jnp.max(s, axis=-1, keepdims=True), (2 * GRID_W, 2 * NA_HEAD_DIM))
        if independent:
            ref, c, wrow = independent.pop(0)
            out_ref[0] += _dot(ref[0, :, c:c + MXU_DEPTH], w_ref[wrow:wrow + MXU_DEPTH, :])
        for p in range(NA_PAIRS):
            lo, hi = p * 2 * NA_HEAD_DIM, (p + 1) * 2 * NA_HEAD_DIM
            m = m_scr[p]
            e = jnp.concatenate(
                [jnp.exp2(s_scr[p, :, c * 128:(c + 1) * 128] - m) for c in range(win_keys // 128)], axis=1)
            o2 = _dot(e.astype(BF16), vwin[pl.ds(local, win_keys), lo:hi])
            o2 = o2 / jnp.sum(e, axis=-1, keepdims=True)
            o = jnp.where(first_head, o2[:GRID_W], o2[GRID_W:])
            gate = sgb_ref[0, qrows, lo:hi].astype(F32)
            yb_scr[qrows, lo:hi] = (o * gate).astype(BF16)
    assert not independent

    out_ref[0] += _dot(yb_scr[...], w_ref[D_MODEL:2 * D_MODEL, :])


def _even_back(x, q, k, v, sgb, ya, ym, ubias, w_out):
    bsz, t_len, _ = x.shape
    rbt = NA_ROW_BLOCK * GRID_W
    nblk = t_len // rbt
    cur = lambda width: pl.BlockSpec((1, rbt, width), lambda b, j: (b, j, 0))
    first = pl.BlockSpec((1, rbt, D_MODEL), lambda b, j: (b, 0, 0))
    nxt = pl.BlockSpec((1, rbt, D_MODEL), lambda b, j: (b, jnp.minimum(j + 1, nblk - 1), 0))
    return pl.pallas_call(
        _even_back_kernel,
        grid=(bsz, nblk),
        in_specs=[cur(D_MODEL), first, nxt, first, nxt,
                  cur(D_MODEL), cur(D_MODEL), cur(MEM_W), cur(D_MODEL),
                  _const_spec(ubias.shape), _const_spec(w_out.shape)],
        out_specs=cur(D_MODEL),
        out_shape=jax.ShapeDtypeStruct((bsz, t_len, D_MODEL), F32),
        scratch_shapes=[pltpu.VMEM((3 * rbt, D_MODEL), BF16), pltpu.VMEM((3 * rbt, D_MODEL), BF16),
                        pltpu.VMEM((rbt, D_MODEL), BF16),
                        pltpu.VMEM((NA_PAIRS, 2 * GRID_W, NA_WIN_ROWS * GRID_W), F32),
                        pltpu.VMEM((NA_PAIRS, 2 * GRID_W, 2 * NA_HEAD_DIM), F32)],
        compiler_params=_COMPILER_PARAMS,
        name="even_back",
    )(q, k, k, v, v, sgb, ya, ym, x, ubias, w_out)


def _lru_gates(xc, wg_ref, n):
    return _dot(xc.astype(BF16), wg_ref[n])


def _lru_coeffs(xc, pre, ba, bi, log2_a_scale):
    r = _sigmoid(pre[:, :LRU_BW] + ba)
    gate_i = _sigmoid(pre[:, LRU_BW:] + bi)
    a = jnp.exp2(r * log2_a_scale)
    return a, jnp.sqrt(1.0 - a * a) * (gate_i * xc)


def _softplus_neg(lam):
    return jnp.maximum(-lam, 0.0) + jnp.log(1.0 + jnp.exp(-jnp.abs(lam)))


def _segment_order(tt):
    p = jnp.arange(tt)
    t = (p % SUBLANES) * (tt // SUBLANES) + p // SUBLANES
    return (t[:, None] == jnp.arange(tt)[None, :]).astype(BF16)


def _scan_sublanes(a, b, reverse):
    pos = lax.broadcasted_iota(jnp.int32, a.shape, 0)
    k = 1
    while k < SUBLANES:
        ok = (pos < SUBLANES - k) if reverse else (pos >= k)
        shift = SUBLANES - k if reverse else k
        a_s = jnp.where(ok, pltpu.roll(a, shift, 0), 1.0)
        b_s = jnp.where(ok, pltpu.roll(b, shift, 0), 0.0)
        b = a * b_s + b
        a = a * a_s
        k *= 2
    return a, b


def _lru_scan(a, b, carry, reverse):
    tt, width = a.shape
    seg = tt // SUBLANES
    a = a.reshape(seg, SUBLANES, width)
    b = b.reshape(seg, SUBLANES, width)
    steps = range(seg - 1, -1, -1) if reverse else range(seg)
    a_tot, h_tot = None, None
    for j in steps:
        if a_tot is None:
            a_tot, h_tot = a[j], b[j]
        else:
            a_tot, h_tot = a_tot * a[j], a[j] * h_tot + b[j]
    a_cum, h_cum = _scan_sublanes(a_tot, h_tot, reverse)
    leaving = a_cum * carry + h_cum
    pos = lax.broadcasted_iota(jnp.int32, leaving.shape, 0)
    if reverse:
        h = jnp.where(pos == SUBLANES - 1, carry, pltpu.roll(leaving, SUBLANES - 1, 0))
        carry = leaving[0:1, :]
    else:
        h = jnp.where(pos == 0, carry, pltpu.roll(leaving, 1, 0))
        carry = leaving[SUBLANES - 1:SUBLANES, :]
    hs = [None] * seg
    for j in steps:
        h = a[j] * h + b[j]
        hs[j] = h
    return jnp.concatenate(hs, axis=0), carry


def _odd_front_kernel(x_ref, xp_ref, xn_ref, g_ref, w_ref, cw_ref, cb_ref, wg_ref, ba_ref, bi_ref,
                      lam_ref, sg_ref, sw_ref, sbt_ref, kv_ref, perm_ref,
                      xc_ref, hf_ref, sgc_ref, yd_ref, ym_ref, xr_scr, carry_scr, z_scr, vn_scr):
    tt = x_ref.shape[1]
    seg = tt // SUBLANES
    i = pl.program_id(1)
    h, h_ext = _normed_with_halo(x_ref, xp_ref, xn_ref, g_ref[...])
    hs = _dot(perm_ref[...], h).astype(BF16)

    @pl.when(i == 0)
    def _():
        carry_scr[...] = jnp.zeros_like(carry_scr)

    lhs = jnp.concatenate([hs, h_ext[:HALO], h_ext[HALO + tt:]], axis=0)
    xr = _dot(lhs, w_ref[:, OD_XR:OD_XR + D_MODEL])

    def project(col):
        if col < OD_U:
            z = _dot(hs, w_ref[:, col:col + MXU_DEPTH])
            sgc_ref[0, :, col - OD_GC:col - OD_GC + MXU_DEPTH] = _silu(z).astype(BF16)
        else:
            z_scr[:, col - OD_U:col - OD_U + MXU_DEPTH] = _dot(h, w_ref[:, col:col + MXU_DEPTH])

    def mem_head(hh):
        qcol = OD_QM - OD_U + hh * MEM_HEAD_DIM
        gcol = OD_GM - OD_U + hh * MEM_HEAD_DIM
        _mem_attention_head(z_scr[:, qcol:qcol + MEM_HEAD_DIM], z_scr[:, gcol:gcol + MEM_HEAD_DIM],
                            kv_ref, ym_ref, hh)

    def norm_v():
        vn_scr[...] = _rms(z_scr[:, OD_V - OD_U:OD_GD - OD_U], sg_ref[...]).astype(BF16)

    def sgu_group(grp):
        lo, hi = grp * LRU_BW, (grp + 1) * LRU_BW
        chunks = range(tt // SGU_CHUNK)
        vn = jnp.concatenate([vn_scr[c * SGU_CHUNK:(c + 1) * SGU_CHUNK, lo:hi] for c in chunks], axis=1)
        mixed_all = _dot(sw_ref[grp], vn)
        for c in chunks:
            r0, r1 = c * SGU_CHUNK, (c + 1) * SGU_CHUNK
            mixed = mixed_all[:, c * LRU_BW:(c + 1) * LRU_BW] + sbt_ref[:, grp:grp + 1]
            u = z_scr[r0:r1, lo:hi]
            sgd = _silu(z_scr[r0:r1, OD_GD - OD_U + lo:OD_GD - OD_U + hi])
            yd_ref[0, r0:r1, lo:hi] = (u * mixed * sgd).astype(BF16)

    slices = lambda start: [functools.partial(project, start + c) for c in range(0, D_MODEL, MXU_DEPTH)]
    for col in range(OD_QM, w_ref.shape[1], MXU_DEPTH):
        project(col)
    uvg = slices(OD_V) + slices(OD_U) + slices(OD_GD)
    gating = lambda a: [functools.partial(sgu_group, g) for g in (a, a + 1)]
    side = [
        (uvg[0:3], [functools.partial(mem_head, 0)]),
        (uvg[3:6], [functools.partial(mem_head, 1), norm_v]),
        (uvg[6:9], [functools.partial(mem_head, 2)]),
        (uvg[9:12], [functools.partial(mem_head, 3)]),
        (slices(OD_GC)[0:1], gating(0)),
        (slices(OD_GC)[1:2], gating(2)),
        (slices(OD_GC)[2:3], gating(4)),
        (slices(OD_GC)[3:4], gating(6)),
    ]
    assert len(side) == LRU_BLOCKS and MEM_HEADS == 4

    before =jnp.where(i > 0, xr[tt + HALO - 2:tt + HALO], 0.0)
    after = jnp.where(i < pl.num_programs(1) - 1, xr[tt + HALO:tt + HALO + 1], 0.0)
    sub = lax.broadcasted_iota(jnp.int32, (SUBLANES, 1), 0)
    lead = LRU_CONV_K // 2
    for d in range(lead):
        slab = pltpu.roll(xr[(seg + d - lead) * SUBLANES:(seg + d - lead + 1) * SUBLANES], 1, 0)
        xr_scr[d * SUBLANES:(d + 1) * SUBLANES] = jnp.where(sub == 0, before[d:d + 1], slab)
    xr_scr[lead * SUBLANES:lead * SUBLANES + tt] = xr[:tt]
    slab = pltpu.roll(xr[0:SUBLANES], SUBLANES - 1, 0)
    xr_scr[lead * SUBLANES + tt:] = jnp.where(sub == SUBLANES - 1, after, slab)
    xc = cb_ref[...]
    for j in range(LRU_CONV_K):
        xc = xc + cw_ref[j:j + 1, :] * xr_scr[j * SUBLANES:j * SUBLANES + tt, :]
    xc_ref[0] = xc

    log2_a_scale = (-LRU_C * LOG2E) * _softplus_neg(lam_ref[...])
    pre = _lru_gates(xc[:, 0:LRU_BW], wg_ref, 0)
    for n in range(LRU_BLOCKS):
        lo, hi = n * LRU_BW, (n + 1) * LRU_BW
        cur = pre
        if n + 1 < LRU_BLOCKS:
            pre = _lru_gates(xc[:, hi:hi + LRU_BW], wg_ref, n + 1)
        projections, consumers = side[n]
        for piece in projections:
            piece()
        for piece in consumers:
            piece()
        a, b = _lru_coeffs(xc[:, lo:hi], cur, ba_ref[:, lo:hi], bi_ref[:, lo:hi], log2_a_scale[:, lo:hi])
        hf_ref[0, :, lo:hi], carry_scr[:, lo:hi] = _lru_scan(a, b, carry_scr[:, lo:hi], reverse=False)


def _odd_front(x, g, w_in, conv_w, conv_b, wg, ba, bi, lam, sgu_g, sgu_w, sgu_bt, kv):
    bsz, t_len, _ = x.shape
    tt = min(TOKEN_TILE, t_len)
    act = lambda width: pl.BlockSpec((1, tt, width), lambda b, i: (b, i, 0))
    sd = lambda width, dt: jax.ShapeDtypeStruct((bsz, t_len, width), dt)
    row = _const_spec((1, D_MODEL))
    return pl.pallas_call(
        _odd_front_kernel,
        grid=(bsz, t_len // tt),
        in_specs=_tile_specs(tt, t_len) + [
            row, _const_spec(w_in.shape), _const_spec(conv_w.shape), row,
            _const_spec(wg.shape), row, row, row,
            row, _const_spec(sgu_w.shape), _const_spec(sgu_bt.shape),
            pl.BlockSpec((1, N_MEM, 2 * MEM_W), lambda b, i: (b, 0, 0)),
            _const_spec((tt, tt)),
        ],
        out_specs=[act(D_MODEL)] * 4 + [act(MEM_W)],
        out_shape=[sd(D_MODEL, F32), sd(D_MODEL, F32), sd(D_MODEL, BF16), sd(D_MODEL, BF16), sd(MEM_W, BF16)],
        scratch_shapes=[pltpu.VMEM((tt + (LRU_CONV_K - 1) * SUBLANES, D_MODEL), F32),
                        pltpu.VMEM((1, D_MODEL), F32),
                        pltpu.VMEM((tt, w_in.shape[1] - OD_U), F32),
                        pltpu.VMEM((tt, D_MODEL), BF16)],
        compiler_params=_COMPILER_PARAMS,
        name="odd_front",
    )(x, x, x, g.reshape(1, D_MODEL), w_in, conv_w, conv_b.reshape(1, D_MODEL), wg,
      ba.reshape(1, D_MODEL), bi.reshape(1, D_MODEL), lam.reshape(1, D_MODEL),
      sgu_g.reshape(1, D_MODEL), sgu_w, sgu_bt, kv, _segment_order(tt))


def _odd_back_kernel(xc_ref, hf_ref, sgc_ref, yd_ref, ym_ref, x_ref, wg_ref, ba_ref, bi_ref, lam_ref,
                     w_ref, fg_ref, unperm_ref, out_ref, carry_scr):
    @pl.when(pl.program_id(1) == 0)
    def _():
        carry_scr[...] = jnp.zeros_like(carry_scr)

    independent = [(yd_ref, c, D_MODEL + c) for c in range(0, D_MODEL, MXU_DEPTH)]
    independent += [(ym_ref, c, 2 * D_MODEL + c) for c in range(0, MEM_W, MXU_DEPTH)]
    acc = x_ref[0]
    log2_a_scale = (-LRU_C * LOG2E) * _softplus_neg(lam_ref[...])
    yc = []
    pre = _lru_gates(xc_ref[0, :, 0:LRU_BW], wg_ref, 0)
    for n in range(LRU_BLOCKS):
        lo, hi = n * LRU_BW, (n + 1) * LRU_BW
        cur = pre
        if n + 1 < LRU_BLOCKS:
            pre = _lru_gates(xc_ref[0, :, hi:hi + LRU_BW], wg_ref, n + 1)
        if independent:
            ref, c, wrow = independent.pop(0)
            acc = acc + _dot(ref[0, :, c:c + MXU_DEPTH], w_ref[wrow:wrow + MXU_DEPTH, :])
        a, b = _lru_coeffs(xc_ref[0, :, lo:hi], cur, ba_ref[:, lo:hi], bi_ref[:, lo:hi], log2_a_scale[:, lo:hi])
        hb, carry_scr[:, lo:hi] = _lru_scan(a, b, carry_scr[:, lo:hi], reverse=True)
        yc.append(((hf_ref[0, :, lo:hi] + hb) * sgc_ref[0, :, lo:hi].astype(F32)).astype(BF16))
        if n % 2 == 1:
            yc_time = _dot(unperm_ref[...], jnp.concatenate(yc, axis=1)).astype(BF16)
            acc = acc + _dot(yc_time, w_ref[lo - LRU_BW:hi, :])
            yc = []
    assert not independent
    out_ref[0] = _rms(acc, fg_ref[...])


def _odd_back(x, xc, hf, sgc, yd, ym, wg, ba, bi, lam, w_out, final_g):
    bsz, t_len, _ = x.shape
    tt = min(TOKEN_TILE, t_len)
    nt = t_len // tt
    act = lambda width: pl.BlockSpec((1, tt, width), lambda b, i: (b, nt - 1 - i, 0))
    row = _const_spec((1, D_MODEL))
    return pl.pallas_call(
        _odd_back_kernel,
        grid=(bsz, nt),
        in_specs=[act(D_MODEL)] * 4 + [act(MEM_W), act(D_MODEL),
                  _const_spec(wg.shape), row, row, row,
                  _const_spec(w_out.shape), row, _const_spec((tt, tt))],
        out_specs=act(D_MODEL),
        out_shape=jax.ShapeDtypeStruct((bsz, t_len, D_MODEL), F32),
        scratch_shapes=[pltpu.VMEM((1, D_MODEL), F32)],
        compiler_params=_COMPILER_PARAMS,
        name="odd_back",
    )(xc, hf, sgc, yd, ym, x, wg, ba.reshape(1, D_MODEL), bi.reshape(1, D_MODEL),
      lam.reshape(1, D_MODEL), w_out, final_g.reshape(1, D_MODEL), _segment_order(tt).T)


def _trunk(x, kv, norm_g, ev_w_in, ev_conv_w, ev_conv_b, ubias, ev_w_out,
           od_w_in, od_conv_w, od_conv_b, od_lru_wa, od_lru_ba, od_lru_wi, od_lru_bi, od_lru_lam,
           od_sgu_g, od_sgu_w, od_sgu_bt, od_w_out, final_g):
    ya, q, k, v, sgb, ym = _even_front(x, norm_g[0], ev_w_in, ev_conv_w, ev_conv_b, kv[0])
    x = _even_back(x, q, k, v, sgb, ya, ym, ubias, ev_w_out)
    wg = jnp.concatenate([od_lru_wa, od_lru_wi], axis=-1)
    xc, hf, sgc, yd, ym = _odd_front(x, norm_g[1], od_w_in, od_conv_w, od_conv_b,
                                     wg[0], od_lru_ba[0], od_lru_bi[0], od_lru_lam[0],
                                     od_sgu_g, od_sgu_w, od_sgu_bt, kv[1])
    return _odd_back(x, xc, hf, sgc, yd, ym, wg[1], od_lru_ba[1], od_lru_bi[1],
                     od_lru_lam[1], od_w_out, final_g)


def kernel(x_prompt, x_sample, mem_prompt, mem_sample, norm_g, mem_norm_g, w_mem_kv, ev_w_in, ev_conv_w,
           ev_conv_b, ev_rpb, ev_w_out, od_w_in, od_conv_w, od_conv_b, od_lru_wa, od_lru_ba, od_lru_wi,
           od_lru_bi, od_lru_lam, od_sgu_g, od_sgu_w, od_sgu_b, od_w_out, final_g):
    depth = norm_g.shape[0]
    assert depth == 2 and ev_w_in.shape[0] == 1 and od_w_in.shape[0] == 1
    nb_p, nb_s = mem_prompt.shape[0], mem_sample.shape[0]
    mem_all = jnp.concatenate([mem_prompt.reshape(-1, D_MODEL), mem_sample.reshape(-1, D_MODEL)], axis=0)
    kv = _memkv(mem_all, mem_norm_g, w_mem_kv.astype(BF16)).reshape(depth, nb_p + nb_s, N_MEM, 2 * MEM_W)
    shared = (norm_g, ev_w_in[0].astype(BF16), ev_conv_w[0], ev_conv_b[0], _na_bias_table(ev_rpb[0]),
              ev_w_out[0].astype(BF16), od_w_in[0].astype(BF16), od_conv_w[0], od_conv_b[0],
              od_lru_wa[0].astype(BF16), od_lru_ba[0], od_lru_wi[0].astype(BF16), od_lru_bi[0], od_lru_lam[0],
              od_sgu_g[0], od_sgu_w[0].astype(BF16), od_sgu_b[0].T, od_w_out[0].astype(BF16), final_g)
    y_prompt = _trunk(x_prompt, kv[:, :nb_p], *shared)
    y_sample = _trunk(x_sample, kv[:, nb_p:], *shared)
    return (y_prompt, y_sample)
```
